```python
import math
import jax, jax.numpy as jnp
from jax import lax
import numpy as np

D_MODEL = 1024
BATCH = 16
SEQ = 4096
DEPTH = 2

N_A_LAYERS = DEPTH // 2
N_B_LAYERS = DEPTH - N_A_LAYERS
HEAD_DIM = 64
MEM_LEN = 256
MEM_HEADS = 4
MEM_WIDTH = MEM_HEADS * HEAD_DIM
MIX_WIDTH = D_MODEL
TOK_WIDTH = MIX_WIDTH - MEM_WIDTH
POOL_WINDOWS = (2, 4, 8, 16)
POOL_GROUP = TOK_WIDTH // len(POOL_WINDOWS)
FOX_HEADS = TOK_WIDTH // HEAD_DIM
Q_BLOCK = 128
D_FF = ((8 * D_MODEL // 3 + 63) // 64) * 64
CONV_WIDTH = 3
DN_ALPHA = (2.0 * DEPTH) ** 0.25
DN_BETA = (8.0 * DEPTH) ** -0.25
LN_EPS = 1e-5

kernel_name = "yoco_pool_fox_memory_convffn"


def layer_norm(x, g, b):
    xf = x.astype(jnp.float32)
    mu = jnp.mean(xf, axis=-1, keepdims=True)
    var = jnp.mean(jnp.square(xf - mu), axis=-1, keepdims=True)
    y = (xf - mu) * lax.rsqrt(var + LN_EPS) * g.astype(jnp.float32) + b.astype(jnp.float32)
    return y.astype(x.dtype)


def causal_mean_minus_self(u, w):
    S = u.shape[1]
    uf = u.astype(jnp.float32)
    csum = jnp.cumsum(uf, axis=1)
    lag = jnp.pad(csum, ((0, 0), (w, 0), (0, 0)))[:, :S]
    count = jnp.minimum(jnp.arange(1, S + 1), w).astype(jnp.float32)[None, :, None]
    return ((csum - lag) / count - uf).astype(u.dtype)


def multiscale_pool(u, pool_w, pool_scale):
    B, S, _ = u.shape
    ug = u.reshape(B, S, len(POOL_WINDOWS), POOL_GROUP)
    pooled = jnp.stack([causal_mean_minus_self(ug[:, :, i], w)
                        for i, w in enumerate(POOL_WINDOWS)], axis=2)
    mixed = jnp.einsum('bsgc,gcd->bsgd', pooled, pool_w)
    return mixed.reshape(B, S, TOK_WIDTH) * pool_scale


def memory_attention(q_mem, mem_k, mem_v):
    B, S = q_mem.shape[:2]
    logits = jnp.einsum('bshd,bmhd->bhsm', q_mem, mem_k).astype(jnp.float32) * (HEAD_DIM ** -0.5)
    p = jax.nn.softmax(logits, axis=-1)
    out = jnp.einsum('bhsm,bmhd->bshd', p.astype(mem_v.dtype), mem_v)
    return out.reshape(B, S, MEM_WIDTH)


def forgetting_attention(q, k, v, F):
    B, S, H, Dh = q.shape
    nb = S // Q_BLOCK
    scale = Dh ** -0.5
    qb = q.reshape(B, nb, Q_BLOCK, H, Dh).transpose(1, 0, 2, 3, 4)
    Fqb = F.reshape(B, nb, Q_BLOCK, H).transpose(1, 0, 3, 2)
    Fk = F.transpose(0, 2, 1)[:, :, None, :]
    k_pos = jnp.arange(S)

    def block(args):
        qi, Fqi, start = args
        logits = jnp.einsum('bqhd,bkhd->bhqk', qi, k).astype(jnp.float32) * scale
        logits = logits + Fqi[..., None] - Fk
        q_pos = start + jnp.arange(Q_BLOCK)
        mask = q_pos[:, None] >= k_pos[None, :]
        logits = jnp.where(mask, logits, -jnp.inf)
        p = jax.nn.softmax(logits, axis=-1)
        return jnp.einsum('bhqk,bkhd->bqhd', p.astype(v.dtype), v)

    starts = jnp.arange(nb) * Q_BLOCK
    out = lax.map(block, (qb, Fqb, starts))
    return out.transpose(1, 0, 2, 3, 4).reshape(B, S, H * Dh)


def conv_ffn(x, w_up, conv_w, conv_b, w_down):
    h = x @ w_up
    C = h.shape[-1]
    h = lax.conv_general_dilated(h, conv_w[:, None, :].astype(h.dtype), window_strides=(1,),
                                 padding=[(CONV_WIDTH - 1, 0)],
                                 dimension_numbers=('NWC', 'WIO', 'NWC'),
                                 feature_group_count=C) + conv_b
    u, g = jnp.split(h, 2, axis=-1)
    return (jax.nn.silu(g) * u) @ w_down


def setup_inputs(seed: int = 0) -> dict:
    key = jax.random.key(seed)
    ks = jax.random.split(key, 24)
    f32 = jnp.float32

    def nrm(k, shape, fan_in, gain=1.0):
        return jax.random.normal(k, shape, f32) * (gain * fan_in ** -0.5)

    x = jax.random.normal(ks[0], (BATCH, SEQ, D_MODEL), f32)
    mem = jax.random.normal(ks[1], (BATCH, MEM_LEN, D_MODEL), f32)
    a_w_in = nrm(ks[2], (N_A_LAYERS, D_MODEL, MIX_WIDTH), D_MODEL)
    a_pool_w = nrm(ks[3], (N_A_LAYERS, len(POOL_WINDOWS), POOL_GROUP, POOL_GROUP), POOL_GROUP)
    a_pool_scale = 1.0 + 0.1 * jax.random.normal(ks[4], (N_A_LAYERS, TOK_WIDTH), f32)
    a_w_out = nrm(ks[5], (N_A_LAYERS, MIX_WIDTH, D_MODEL), MIX_WIDTH, DN_BETA)
    b_w_q = nrm(ks[6], (N_B_LAYERS, D_MODEL, MIX_WIDTH), D_MODEL)
    b_w_out = nrm(ks[7], (N_B_LAYERS, MIX_WIDTH, D_MODEL), MIX_WIDTH, DN_BETA)
    kv_w = jnp.concatenate([nrm(ks[8], (D_MODEL, 2 * TOK_WIDTH), D_MODEL),
                            nrm(ks[9], (D_MODEL, FOX_HEADS), D_MODEL, 0.5)], axis=-1)
    f_b = jax.random.uniform(ks[10], (FOX_HEADS,), f32, 1.0, 4.0)
    mem_w_kv = nrm(ks[11], (DEPTH, D_MODEL, 2 * MEM_WIDTH), D_MODEL)
    ln1_g = 1.0 + 0.05 * jax.random.normal(ks[12], (DEPTH, D_MODEL), f32)
    ln1_b = 0.02 * jax.random.normal(ks[13], (DEPTH, D_MODEL), f32)
    ln2_g = 1.0 + 0.05 * jax.random.normal(ks[14], (DEPTH, D_MODEL), f32)
    ln2_b = 0.02 * jax.random.normal(ks[15], (DEPTH, D_MODEL), f32)
    ffn_w_up = nrm(ks[16], (DEPTH, D_MODEL, 2 * D_FF), D_MODEL)
    ffn_conv_w = nrm(ks[17], (DEPTH, CONV_WIDTH, 2 * D_FF), CONV_WIDTH)
    ffn_conv_b = 0.02 * jax.random.normal(ks[18], (DEPTH, 2 * D_FF), f32)
    ffn_w_down = nrm(ks[19], (DEPTH, D_FF, D_MODEL), D_FF, DN_BETA)
    return {"x": x, "mem": mem, "a_w_in": a_w_in, "a_pool_w": a_pool_w,
            "a_pool_scale": a_pool_scale, "a_w_out": a_w_out, "b_w_q": b_w_q,
            "b_w_out": b_w_out, "kv_w": kv_w, "f_b": f_b, "mem_w_kv": mem_w_kv,
            "ln1_g": ln1_g, "ln1_b": ln1_b, "ln2_g": ln2_g, "ln2_b": ln2_b,
            "ffn_w_up": ffn_w_up, "ffn_conv_w": ffn_conv_w, "ffn_conv_b": ffn_conv_b,
            "ffn_w_down": ffn_w_down}


def reference(x, mem, a_w_in, a_pool_w, a_pool_scale, a_w_out, b_w_q, b_w_out, kv_w, f_b,
              mem_w_kv, ln1_g, ln1_b, ln2_g, ln2_b, ffn_w_up, ffn_conv_w, ffn_conv_b,
              ffn_w_down):
    B, S, _ = x.shape
    M = mem.shape[1]
    k_sh = v_sh = F_sh = None
    for l in range(DEPTH):
        mem_kv = mem @ mem_w_kv[l]
        mem_k = mem_kv[..., :MEM_WIDTH].reshape(B, M, MEM_HEADS, HEAD_DIM)
        mem_v = mem_kv[..., MEM_WIDTH:].reshape(B, M, MEM_HEADS, HEAD_DIM)

        if l < N_A_LAYERS:
            proj = x @ a_w_in[l]
            tok = multiscale_pool(proj[..., :TOK_WIDTH], a_pool_w[l], a_pool_scale[l])
            w_out = a_w_out[l]
        else:
            if l == N_A_LAYERS:
                kvf = x @ kv_w
                k_sh = kvf[..., :TOK_WIDTH].reshape(B, S, FOX_HEADS, HEAD_DIM)
                v_sh = kvf[..., TOK_WIDTH:2 * TOK_WIDTH].reshape(B, S, FOX_HEADS, HEAD_DIM)
                log_f = jax.nn.log_sigmoid(kvf[..., 2 * TOK_WIDTH:].astype(jnp.float32)
                                           + f_b.astype(jnp.float32))
                F_sh = jnp.cumsum(log_f, axis=1)
            j = l - N_A_LAYERS
            proj = x @ b_w_q[j]
            q = proj[..., :TOK_WIDTH].reshape(B, S, FOX_HEADS, HEAD_DIM)
            tok = forgetting_attention(q, k_sh, v_sh, F_sh)
            w_out = b_w_out[j]

        q_mem = proj[..., TOK_WIDTH:].reshape(B, S, MEM_HEADS, HEAD_DIM)
        mem_out = memory_attention(q_mem, mem_k, mem_v)
        mix = jnp.concatenate([tok, mem_out], axis=-1) @ w_out
        x = layer_norm(DN_ALPHA * x + mix, ln1_g[l], ln1_b[l])

        ffn = conv_ffn(x, ffn_w_up[l], ffn_conv_w[l], ffn_conv_b[l], ffn_w_down[l])
        x = layer_norm(DN_ALPHA * x + ffn, ln2_g[l], ln2_b[l])
    return x
```

```python
import functools

import jax
import jax.numpy as jnp
from jax import lax
from jax.experimental import pallas as pl
from jax.experimental.pallas import tpu as pltpu

F32 = jnp.float32
BF16 = jnp.bfloat16

HEAD_DIM = 64
MEM_HEADS = 4
MEM_WIDTH = MEM_HEADS * HEAD_DIM
POOL_WINDOWS = (2, 4, 8, 16)
N_POOL = len(POOL_WINDOWS)
CONV_WIDTH = 3
LN_EPS = 1e-5
DEPTH = 2
DN_ALPHA = (2.0 * DEPTH) ** 0.25
ATTN_SCALE = HEAD_DIM ** -0.5

LANES = 128
SUBLANES = 8
MXU_N = 256
VMEM_LIMIT_BYTES = 56 * 1024 * 1024

POOL_PAD = MXU_N
POOL_HALO = 16
DECAY_PARTS = 3
DECAY_LANES = 2 * DECAY_PARTS
NEG_BIG = -1e30


def _round_up(n, m):
    return (n + m - 1) // m * m


def _const_spec(shape):
    nd = len(shape)
    return pl.BlockSpec(shape, lambda *_: (0,) * nd, pipeline_mode=pl.Buffered(1))


def _params(n_grid):
    return pltpu.CompilerParams(dimension_semantics=("arbitrary",) * n_grid,
                                vmem_limit_bytes=VMEM_LIMIT_BYTES)


def _layer_norm(y, g, b):
    mu = jnp.mean(y, axis=-1, keepdims=True)
    d = y - mu
    var = jnp.mean(d * d, axis=-1, keepdims=True)
    return d * lax.rsqrt(var + LN_EPS) * g + b


def _dot(a, b):
    return jnp.dot(a, b, preferred_element_type=F32)


def _dot_nt(a, b):
    return lax.dot_general(a, b, (((1,), (1,)), ((), ())), preferred_element_type=F32)


def _split3(x):
    hi = x.astype(BF16)
    r = x - hi.astype(F32)
    mid = r.astype(BF16)
    lo = (r - mid.astype(F32)).astype(BF16)
    return hi, mid, lo


def _memkv_kernel(mem_ref, w_ref, k_ref, v_ref):
    m_len = mem_ref.shape[1]
    kv = _dot(mem_ref[0].astype(BF16), w_ref[0])
    k = kv[:, :MEM_WIDTH] * ATTN_SCALE
    v = kv[:, MEM_WIDTH:]
    head_of_lane = lax.broadcasted_iota(jnp.int32, (m_len, MEM_WIDTH), 1) // HEAD_DIM
    for h in range(MEM_HEADS):
        sel = head_of_lane == h
        k_ref[0, 0, h * m_len:(h + 1) * m_len, :] = jnp.where(sel, k, 0.0).astype(BF16)
        v_ref[0, 0, h * m_len:(h + 1) * m_len, :] = jnp.where(sel, v, 0.0).astype(BF16)


def _memkv(mem, w_kv_bf16):
    B, M, D = mem.shape
    L = w_kv_bf16.shape[0]
    out = jax.ShapeDtypeStruct((L, B, MEM_HEADS * M, MEM_WIDTH), BF16)
    out_spec = pl.BlockSpec((1, 1, MEM_HEADS * M, MEM_WIDTH), lambda l, b: (l, b, 0, 0))
    return pl.pallas_call(
        _memkv_kernel,
        grid=(L, B),
        in_specs=[pl.BlockSpec((1, M, D), lambda l, b: (b, 0, 0)),
                  pl.BlockSpec((1, D, 2 * MEM_WIDTH), lambda l, b: (l, 0, 0))],
        out_specs=[out_spec, out_spec],
        out_shape=[out, out],
        compiler_params=_params(2),
        name="memkv",
    )(mem, w_kv_bf16)


def _memory_attention(q_mem_bf16, k_big, v_big, m_len):
    logits = _dot_nt(q_mem_bf16, k_big)
    probs = []
    for h in range(MEM_HEADS):
        lh = logits[:, h * m_len:(h + 1) * m_len]
        e = jnp.exp(lh - jnp.max(lh, axis=-1, keepdims=True))
        p = e / jnp.sum(e, axis=-1, keepdims=True)
        probs.append(p.astype(BF16))
    return _dot(jnp.concatenate(probs, axis=1), v_big)


def _mix_a_kernel(x_ref, w_in_ref, pool_w_ref, pool_s_ref, kb_ref, vb_ref, w_out_ref, g_ref, b_ref,
                  o_ref, hist_ref, cat_ref):
    ts = x_ref.shape[1]
    m_len = kb_ref.shape[1] // MEM_HEADS
    tok_w = N_POOL * POOL_PAD
    s = pl.program_id(1)
    x = x_ref[0]
    proj = _dot(x.astype(BF16), w_in_ref[...])

    @pl.when(s == 0)
    def _():
        hist_ref[0:POOL_HALO, :] = jnp.zeros((POOL_HALO, tok_w), F32)

    @pl.when(s > 0)
    def _():
        hist_ref[0:POOL_HALO, :] = hist_ref[ts:ts + POOL_HALO, :]

    hist_ref[POOL_HALO:POOL_HALO + ts, :] = proj[:, :tok_w]

    pos = s * ts + lax.broadcasted_iota(jnp.int32, (ts, 1), 0)
    for i, w in enumerate(POOL_WINDOWS):
        cols = slice(i * POOL_PAD, (i + 1) * POOL_PAD)
        u = hist_ref[POOL_HALO:POOL_HALO + ts, cols]
        win = u
        for j in range(1, w):
            win = win + hist_ref[POOL_HALO - j:POOL_HALO - j + ts, cols]
        inv_count = 1.0 / jnp.minimum(pos + 1, w).astype(F32)
        pooled = win * inv_count - u
        mixed = _dot(pooled.astype(BF16), pool_w_ref[i]) * pool_s_ref[i:i + 1, :]
        cat_ref[:, cols] = mixed.astype(BF16)

    mem_out = _memory_attention(proj[:, tok_w:].astype(BF16), kb_ref[0], vb_ref[0], m_len)
    cat_ref[:, tok_w:] = mem_out.astype(BF16)

    mix = _dot(cat_ref[...], w_out_ref[...])
    o_ref[0] = _layer_norm(DN_ALPHA * x + mix, g_ref[...], b_ref[...])


def _mix_a(x, w_in_pad, pool_w_pad, pool_s_pad, k_big, v_big, w_out_pad, g, b, ts):
    B, S, D = x.shape
    wp = w_in_pad.shape[1]
    tok_w = N_POOL * POOL_PAD
    mm = k_big.shape[1]
    return pl.pallas_call(
        _mix_a_kernel,
        grid=(B, S // ts),
        in_specs=[pl.BlockSpec((1, ts, D), lambda bi, si: (bi, si, 0)),
                  _const_spec((D, wp)),
                  _const_spec((N_POOL, POOL_PAD, POOL_PAD)),
                  _const_spec((N_POOL, POOL_PAD)),
                  pl.BlockSpec((1, mm, MEM_WIDTH), lambda bi, si: (bi, 0, 0)),
                  pl.BlockSpec((1, mm, MEM_WIDTH), lambda bi, si: (bi, 0, 0)),
                  _const_spec((wp, D)),
                  _const_spec((1, D)),
                  _const_spec((1, D))],
        out_specs=pl.BlockSpec((1, ts, D), lambda bi, si: (bi, si, 0)),
        out_shape=jax.ShapeDtypeStruct((B, S, D), F32),
        scratch_shapes=[pltpu.VMEM((POOL_HALO + ts, tok_w), F32),
                        pltpu.VMEM((ts, wp), BF16)],
        compiler_params=_params(2),
        name="mix_a",
    )(x, w_in_pad, pool_w_pad, pool_s_pad, k_big, v_big, w_out_pad, g, b)


def _ffn_kernel(x_ref, wu_ref, wg_ref, cp_ref, wd_ref, g_ref, b_ref, o_ref,
                hu_ref, hg_ref, carry_ref, acc_ref):
    ts = x_ref.shape[1]
    n_chunks = wu_ref.shape[0]
    halo = SUBLANES
    s = pl.program_id(1)
    x = x_ref[0]
    xb = x.astype(BF16)

    @pl.when(s == 0)
    def _():
        carry_ref[...] = jnp.zeros(carry_ref.shape, F32)

    acc_ref[...] = jnp.zeros(acc_ref.shape, F32)

    def conv(h_ref, taps, bias):
        return (h_ref[halo - 2:halo - 2 + ts, :] * taps[0:1]
                + h_ref[halo - 1:halo - 1 + ts, :] * taps[1:2]
                + h_ref[halo:halo + ts, :] * taps[2:3] + bias)

    def chunk(c, carry):
        hu_ref[0:halo, :] = carry_ref[0, c]
        hg_ref[0:halo, :] = carry_ref[1, c]
        hu_ref[halo:halo + ts, :] = _dot(xb, wu_ref[c])
        hg_ref[halo:halo + ts, :] = _dot(xb, wg_ref[c])
        carry_ref[0, c] = hu_ref[ts:ts + halo, :]
        carry_ref[1, c] = hg_ref[ts:ts + halo, :]
        cp = cp_ref[c]
        u = conv(hu_ref, cp[0:3], cp[6:7])
        gate = conv(hg_ref, cp[3:6], cp[7:8])
        act = gate * (1.0 / (1.0 + jnp.exp(-gate))) * u
        acc_ref[...] += _dot(act.astype(BF16), wd_ref[c])
        return carry

    lax.fori_loop(0, n_chunks, chunk, 0)
    o_ref[0] = _layer_norm(DN_ALPHA * x + acc_ref[...], g_ref[...], b_ref[...])


def _ffn(x, wu3, wg3, cp3, wd3, g, b, ts):
    B, S, D = x.shape
    n_chunks, _, fc = wu3.shape
    return pl.pallas_call(
        _ffn_kernel,
        grid=(B, S // ts),
        in_specs=[pl.BlockSpec((1, ts, D), lambda bi, si: (bi, si, 0)),
                  _const_spec(wu3.shape),
                  _const_spec(wg3.shape),
                  _const_spec(cp3.shape),
                  _const_spec(wd3.shape),
                  _const_spec((1, D)),
                  _const_spec((1, D))],
        out_specs=pl.BlockSpec((1, ts, D), lambda bi, si: (bi, si, 0)),
        out_shape=jax.ShapeDtypeStruct((B, S, D), F32),
        scratch_shapes=[pltpu.VMEM((SUBLANES + ts, fc), F32),
                        pltpu.VMEM((SUBLANES + ts, fc), F32),
                        pltpu.VMEM((2, n_chunks, SUBLANES, fc), F32),
                        pltpu.VMEM((ts, D), F32)],
        compiler_params=_params(2),
        name="ffn",
    )(x, wu3, wg3, cp3, wd3, g, b)


def _qkv_kernel(x_ref, w_ref, fb_ref, tri_ref, pq_ref, pk_ref, cq_ref, ck_ref,
                q_ref, qm_ref, k_ref, v_ref, fq_ref, fk_ref, carry_ref):
    ts = x_ref.shape[1]
    tok_w = q_ref.shape[2]
    s = pl.program_id(1)
    proj = _dot(x_ref[0].astype(BF16), w_ref[...])
    o = 0
    q_ref[0] = proj[:, o:o + tok_w].astype(BF16)
    o += tok_w
    qm_ref[0] = proj[:, o:o + MEM_WIDTH].astype(BF16)
    o += MEM_WIDTH
    k_ref[0] = proj[:, o:o + tok_w].astype(BF16)
    o += tok_w
    v_ref[0] = proj[:, o:o + tok_w].astype(BF16)
    o += tok_w

    @pl.when(s == 0)
    def _():
        carry_ref[...] = jnp.zeros(carry_ref.shape, F32)

    z = proj[:, o:o + LANES] + fb_ref[...]
    log_f = jnp.minimum(z, 0.0) - jnp.log1p(jnp.exp(-jnp.abs(z)))
    part = _dot(tri_ref[...], jnp.concatenate(_split3(log_f), axis=1))
    cum = (part[:, 0:LANES] + part[:, LANES:2 * LANES] + part[:, 2 * LANES:]) + carry_ref[0:1, :]
    carry_ref[0:1, :] = cum[ts - 1:ts, :]
    pieces = jnp.concatenate(_split3(cum), axis=1)
    fq_ref[0] = (_dot(pieces, pq_ref[...]) + cq_ref[...]).astype(BF16)
    fk_ref[0] = (_dot(pieces, pk_ref[...]) + ck_ref[...]).astype(BF16)


def _qkv(x, w_all, fb_pad, tri, pq, pk, cq, ck, tok_w, ts):
    B, S, D = x.shape
    row = lambda width: pl.BlockSpec((1, ts, width), lambda bi, si: (bi, si, 0))
    shp = lambda width: jax.ShapeDtypeStruct((B, S, width), BF16)
    return pl.pallas_call(
        _qkv_kernel,
        grid=(B, S // ts),
        in_specs=[pl.BlockSpec((1, ts, D), lambda bi, si: (bi, si, 0)),
                  _const_spec(w_all.shape), _const_spec(fb_pad.shape), _const_spec(tri.shape),
                  _const_spec(pq.shape), _const_spec(pk.shape),
                  _const_spec(cq.shape), _const_spec(ck.shape)],
        out_specs=[row(tok_w), row(MEM_WIDTH), row(tok_w), row(tok_w), row(LANES), row(LANES)],
        out_shape=[shp(tok_w), shp(MEM_WIDTH), shp(tok_w), shp(tok_w), shp(LANES), shp(LANES)],
        scratch_shapes=[pltpu.VMEM((SUBLANES, LANES), F32)],
        compiler_params=_params(2),
        name="qkv",
    )(x, w_all, fb_pad, tri, pq, pk, cq, ck)


def _fox_kernel(q_ref, fq_ref, k_ref, fk_ref, v_ref, o_ref, kaug_ref):
    tq = q_ref.shape[1]
    seq = k_ref.shape[1]
    j = pl.program_id(1)
    i = pl.program_id(2)

    @pl.when(i == 0)
    def _():
        lane = lax.broadcasted_iota(jnp.int32, (seq, LANES), 1)
        kp = k_ref[0]
        fk = fk_ref[0]
        zero = jnp.zeros((seq, LANES), BF16)
        for h in range(2):
            head = 2 * j + h
            kaug_ref[h, :, 0:LANES] = jnp.where(lane // HEAD_DIM == h, kp, zero)
            in_head = (lane >= head * DECAY_LANES) & (lane < (head + 1) * DECAY_LANES)
            kaug_ref[h, :, LANES:] = jnp.where(in_head, fk, zero)

    qa = jnp.concatenate([q_ref[0], fq_ref[0]], axis=1)
    row = lax.broadcasted_iota(jnp.int32, (tq, tq), 0)
    col = lax.broadcasted_iota(jnp.int32, (tq, tq), 1)
    causal = row >= col
    diag = pl.multiple_of(i * tq, tq)

    outs = []
    for h in range(2):
        sc = jnp.where(causal, _dot_nt(qa, kaug_ref[h, pl.ds(diag, tq), :]), NEG_BIG)
        m0 = jnp.max(sc, axis=-1, keepdims=True)
        p = jnp.exp(sc - m0)
        l0 = jnp.sum(p, axis=-1, keepdims=True)
        acc0 = _dot(p.astype(BF16), v_ref[0, pl.ds(diag, tq), :])

        def kstep(t, carry, h=h):
            m, l, acc = carry
            start = pl.multiple_of(t * tq, tq)
            sc = _dot_nt(qa, kaug_ref[h, pl.ds(start, tq), :])
            m_new = jnp.maximum(m, jnp.max(sc, axis=-1, keepdims=True))
            a = jnp.exp(m - m_new)
            p = jnp.exp(sc - m_new)
            l = a * l + jnp.sum(p, axis=-1, keepdims=True)
            acc = a * acc + _dot(p.astype(BF16), v_ref[0, pl.ds(start, tq), :])
            return m_new, l, acc

        _, l, acc = lax.fori_loop(0, i, kstep, (m0, l0, acc0))
        outs.append(acc / l)

    lane = lax.broadcasted_iota(jnp.int32, (tq, LANES), 1)
    o_ref[0] = jnp.where(lane < HEAD_DIM, outs[0], outs[1]).astype(BF16)


def _fox(q, fq, k, fk, v, tq):
    B, S, tok_w = q.shape
    n_pairs = tok_w // LANES
    return pl.pallas_call(
        _fox_kernel,
        grid=(B, n_pairs, S // tq),
        in_specs=[pl.BlockSpec((1, tq, LANES), lambda b, j, i: (b, i, j)),
                  pl.BlockSpec((1, tq, LANES), lambda b, j, i: (b, i, 0)),
                  pl.BlockSpec((1, S, LANES), lambda b, j, i: (b, 0, j)),
                  pl.BlockSpec((1, S, LANES), lambda b, j, i: (b, 0, 0)),
                  pl.BlockSpec((1, S, LANES), lambda b, j, i: (b, 0, j))],
        out_specs=pl.BlockSpec((1, tq, LANES), lambda b, j, i: (b, i, j)),
        out_shape=jax.ShapeDtypeStruct((B, S, tok_w), BF16),
        scratch_shapes=[pltpu.VMEM((2, S, 2 * LANES), BF16)],
        compiler_params=_params(3),
        name="fox",
    )(q, fq, k, fk, v)


def _mix_b_kernel(x_ref, tok_ref, qm_ref, kb_ref, vb_ref, w_tok_ref, w_mem_ref, g_ref, b_ref, o_ref):
    m_len = kb_ref.shape[1] // MEM_HEADS
    mem_out = _memory_attention(qm_ref[0], kb_ref[0], vb_ref[0], m_len)
    mix = _dot(tok_ref[0], w_tok_ref[...]) + _dot(mem_out.astype(BF16), w_mem_ref[...])
    o_ref[0] = _layer_norm(DN_ALPHA * x_ref[0] + mix, g_ref[...], b_ref[...])


def _mix_b(x, tok, qm, k_big, v_big, w_tok, w_mem, g, b, ts):
    B, S, D = x.shape
    tok_w = tok.shape[2]
    mm = k_big.shape[1]
    return pl.pallas_call(
        _mix_b_kernel,
        grid=(B, S // ts),
        in_specs=[pl.BlockSpec((1, ts, D), lambda bi, si: (bi, si, 0)),
                  pl.BlockSpec((1, ts, tok_w), lambda bi, si: (bi, si, 0)),
                  pl.BlockSpec((1, ts, MEM_WIDTH), lambda bi, si: (bi, si, 0)),
                  pl.BlockSpec((1, mm, MEM_WIDTH), lambda bi, si: (bi, 0, 0)),
                  pl.BlockSpec((1, mm, MEM_WIDTH), lambda bi, si: (bi, 0, 0)),
                  _const_spec(w_tok.shape), _const_spec(w_mem.shape),
                  _const_spec((1, D)), _const_spec((1, D))],
        out_specs=pl.BlockSpec((1, ts, D), lambda bi, si: (bi, si, 0)),
        out_shape=jax.ShapeDtypeStruct((B, S, D), F32),
        compiler_params=_params(2),
        name="mix_b",
    )(x, tok, qm, k_big, v_big, w_tok, w_mem, g, b)


def _prep_ffn(w_up, conv_w, conv_b, w_down, fc):
    d, two_f = w_up.shape
    f = two_f // 2
    fp = _round_up(f, fc)
    n_chunks = fp // fc
    pad_cols = lambda a: jnp.pad(a, ((0, 0), (0, fp - f)))
    chunked = lambda a: a.reshape(a.shape[0], n_chunks, fc).transpose(1, 0, 2)
    wu3 = chunked(pad_cols(w_up[:, :f])).astype(BF16)
    wg3 = chunked(pad_cols(w_up[:, f:])).astype(BF16)
    rows = jnp.concatenate([conv_w[:, :f], conv_w[:, f:], conv_b[None, :f], conv_b[None, f:]], axis=0)
    cp3 = chunked(pad_cols(rows))
    wd3 = jnp.pad(w_down, ((0, fp - f), (0, 0))).reshape(n_chunks, fc, d).astype(BF16)
    return wu3, wg3, cp3, wd3


def _prep_mix_a(w_in, pool_w, pool_scale, w_out):
    d = w_in.shape[0]
    grp = pool_w.shape[1]
    tok_w = N_POOL * grp
    pad = POOL_PAD - grp
    w_tok = jnp.pad(w_in[:, :tok_w].reshape(d, N_POOL, grp), ((0, 0), (0, 0), (0, pad)))
    w_in_pad = jnp.concatenate([w_tok.reshape(d, N_POOL * POOL_PAD), w_in[:, tok_w:]], axis=1)
    pool_w_pad = jnp.pad(pool_w, ((0, 0), (0, pad), (0, pad)))
    pool_s_pad = jnp.pad(pool_scale.reshape(N_POOL, grp), ((0, 0), (0, pad)))
    o_tok = jnp.pad(w_out[:tok_w].reshape(N_POOL, grp, d), ((0, 0), (0, pad), (0, 0)))
    w_out_pad = jnp.concatenate([o_tok.reshape(N_POOL * POOL_PAD, d), w_out[tok_w:]], axis=0)
    return w_in_pad.astype(BF16), pool_w_pad.astype(BF16), pool_s_pad, w_out_pad.astype(BF16)


def _prep_qkv(w_q, kv_w, f_b, tok_w, ts):
    n_heads = tok_w // HEAD_DIM
    assert n_heads * DECAY_LANES <= LANES
    w_f = jnp.pad(kv_w[:, 2 * tok_w:], ((0, 0), (0, LANES - n_heads)))
    w_all = jnp.concatenate([w_q[:, :tok_w] * ATTN_SCALE, w_q[:, tok_w:], kv_w[:, :2 * tok_w], w_f],
                            axis=1).astype(BF16)
    fb_pad = jnp.pad(f_b, (0, LANES - n_heads)).reshape(1, LANES)
    tri = (jnp.arange(ts)[:, None] >= jnp.arange(ts)[None, :]).astype(BF16)
    src = jnp.arange(DECAY_PARTS * LANES)
    part, head = src // LANES, src % LANES
    dst = jnp.arange(LANES)
    valid = (head < n_heads)[:, None]
    pq = (valid & (dst[None, :] == (head * DECAY_LANES + part)[:, None])).astype(BF16)
    pk = -(valid & (dst[None, :] == (head * DECAY_LANES + DECAY_PARTS + part)[:, None])).astype(BF16)
    in_use = dst < n_heads * DECAY_LANES
    cq = (in_use & (dst % DECAY_LANES >= DECAY_PARTS)).astype(F32).reshape(1, LANES)
    ck = (in_use & (dst % DECAY_LANES < DECAY_PARTS)).astype(F32).reshape(1, LANES)
    return w_all, fb_pad, tri, pq, pk, cq, ck


ROW_TILE = 512
FFN_CHUNK = 256
ATTN_TILE = 256


def kernel(x, mem, a_w_in, a_pool_w, a_pool_scale, a_w_out, b_w_q, b_w_out, kv_w, f_b, mem_w_kv,
           ln1_g, ln1_b, ln2_g, ln2_b, ffn_w_up, ffn_conv_w, ffn_conv_b, ffn_w_down):
    B, S, D = x.shape
    n_a = a_w_in.shape[0]
    n_b = b_w_q.shape[0]
    tok_w = D - MEM_WIDTH
    ts = min(ROW_TILE, S)
    tq = min(ATTN_TILE, S)
    assert S % ts == 0 and S % tq == 0 and tok_w % LANES == 0

    k_big, v_big = _memkv(mem, mem_w_kv.astype(BF16))
    row = lambda a, l: a[l].reshape(1, D)

    shared = None
    for l in range(n_a + n_b):
        if l < n_a:
            w_in_pad, pool_w_pad, pool_s_pad, w_out_pad = _prep_mix_a(
                a_w_in[l], a_pool_w[l], a_pool_scale[l], a_w_out[l])
            x = _mix_a(x, w_in_pad, pool_w_pad, pool_s_pad, k_big[l], v_big[l], w_out_pad,
                       row(ln1_g, l), row(ln1_b, l), ts)
        else:
            j = l - n_a
            w_all, fb_pad, tri, pq, pk, cq, ck = _prep_qkv(b_w_q[j], kv_w, f_b, tok_w, ts)
            q, qm, k_new, v_new, fq_new, fk_new = _qkv(x, w_all, fb_pad, tri, pq, pk, cq, ck, tok_w, ts)
            if j == 0:
                shared = (fq_new, k_new, fk_new, v_new)
            fq, k_sh, fk_sh, v_sh = shared
            tok = _fox(q, fq, k_sh, fk_sh, v_sh, tq)
            w_out = b_w_out[j].astype(BF16)
            x = _mix_b(x, tok, qm, k_big[l], v_big[l], w_out[:tok_w], w_out[tok_w:],
                       row(ln1_g, l), row(ln1_b, l), ts)
        wu3, wg3, cp3, wd3 = _prep_ffn(ffn_w_up[l], ffn_conv_w[l], ffn_conv_b[l], ffn_w_down[l], FFN_CHUNK)
        x = _ffn(x, wu3, wg3, cp3, wd3, row(ln2_g, l), row(ln2_b, l), ts)
    return x
```

```python
import functools

import jax
import jax.numpy as jnp
from jax import lax
from jax.experimental import pallas as pl
from jax.experimental.pallas import tpu as pltpu

F32 = jnp.float32
BF16 = jnp.bfloat16

HEAD_DIM = 64
MEM_HEADS = 4
MEM_WIDTH = MEM_HEADS * HEAD_DIM
POOL_WINDOWS = (2, 4, 8, 16)
N_POOL = len(POOL_WINDOWS)
CONV_WIDTH = 3
LN_EPS = 1e-5
DEPTH = 2
DN_ALPHA = (2.0 * DEPTH) ** 0.25
ATTN_SCALE = HEAD_DIM ** -0.5

LANES = 128
SUBLANES = 8
MXU_N = 256
VMEM_LIMIT_BYTES = 56 * 1024 * 1024

POOL_PAD = MXU_N
POOL_HALO = 16
DECAY_PARTS = 3
DECAY_LANES = 2 * DECAY_PARTS
NEG_BIG = -1e30


def _round_up(n, m):
    return (n + m - 1) // m * m


def _const_spec(shape):
    nd = len(shape)
    return pl.BlockSpec(shape, lambda *_: (0,) * nd, pipeline_mode=pl.Buffered(1))


def _params(n_grid):
    return pltpu.CompilerParams(dimension_semantics=("arbitrary",) * n_grid,
                                vmem_limit_bytes=VMEM_LIMIT_BYTES)


def _layer_norm(y, g, b):
    mu = jnp.mean(y, axis=-1, keepdims=True)
    d = y - mu
    var = jnp.mean(d * d, axis=-1, keepdims=True)
    return d * lax.rsqrt(var + LN_EPS) * g + b


def _dot(a, b):
    return jnp.dot(a, b, preferred_element_type=F32)


def _dot_nt(a, b):
    return lax.dot_general(a, b, (((1,), (1,)), ((), ())), preferred_element_type=F32)


def _split3(x):
    hi = x.astype(BF16)
    r = x - hi.astype(F32)
    mid = r.astype(BF16)
    lo = (r - mid.astype(F32)).astype(BF16)
    return hi, mid, lo


def _memkv_kernel(mem_ref, w_ref, k_ref, v_ref):
    m_len = mem_ref.shape[1]
    kv = _dot(mem_ref[0].astype(BF16), w_ref[0])
    k = kv[:, :MEM_WIDTH] * ATTN_SCALE
    v = kv[:, MEM_WIDTH:]
    head_of_lane = lax.broadcasted_iota(jnp.int32, (m_len, MEM_WIDTH), 1) // HEAD_DIM
    for h in range(MEM_HEADS):
        sel = head_of_lane == h
        k_ref[0, 0, h * m_len:(h + 1) * m_len, :] = jnp.where(sel, k, 0.0).astype(BF16)
        v_ref[0, 0, h * m_len:(h + 1) * m_len, :] = jnp.where(sel, v, 0.0).astype(BF16)


def _memkv(mem, w_kv_bf16):
    B, M, D = mem.shape
    L = w_kv_bf16.shape[0]
    out = jax.ShapeDtypeStruct((L, B, MEM_HEADS * M, MEM_WIDTH), BF16)
    out_spec = pl.BlockSpec((1, 1, MEM_HEADS * M, MEM_WIDTH), lambda l, b: (l, b, 0, 0))
    return pl.pallas_call(
        _memkv_kernel,
        grid=(L, B),
        in_specs=[pl.BlockSpec((1, M, D), lambda l, b: (b, 0, 0)),
                  pl.BlockSpec((1, D, 2 * MEM_WIDTH), lambda l, b: (l, 0, 0))],
        out_specs=[out_spec, out_spec],
        out_shape=[out, out],
        compiler_params=_params(2),
        name="memkv",
    )(mem, w_kv_bf16)


def _memory_attention(q_mem_bf16, k_big, v_big, m_len):
    logits = _dot_nt(q_mem_bf16, k_big)
    probs = []
    for h in range(MEM_HEADS):
        lh = logits[:, h * m_len:(h + 1) * m_len]
        e = jnp.exp(lh - jnp.max(lh, axis=-1, keepdims=True))
        p = e / jnp.sum(e, axis=-1, keepdims=True)
        probs.append(p.astype(BF16))
    return _dot(jnp.concatenate(probs, axis=1), v_big)


def _mix_a_kernel(x_ref, w_in_ref, pool_w_ref, pool_s_ref, kb_ref, vb_ref, w_out_ref, g_ref, b_ref,
                  o_ref, hist_ref, cat_ref):
    ts = x_ref.shape[1]
    m_len = kb_ref.shape[1] // MEM_HEADS
    tok_w = N_POOL * POOL_PAD
    s = pl.program_id(1)
    x = x_ref[0]
    proj = _dot(x.astype(BF16), w_in_ref[...])

    @pl.when(s == 0)
    def _():
        hist_ref[0:POOL_HALO, :] = jnp.zeros((POOL_HALO, tok_w), F32)

    @pl.when(s > 0)
    def _():
        hist_ref[0:POOL_HALO, :] = hist_ref[ts:ts + POOL_HALO, :]

    hist_ref[POOL_HALO:POOL_HALO + ts, :] = proj[:, :tok_w]

    pos = s * ts + lax.broadcasted_iota(jnp.int32, (ts, 1), 0)
    for i, w in enumerate(POOL_WINDOWS):
        cols = slice(i * POOL_PAD, (i + 1) * POOL_PAD)
        u = hist_ref[POOL_HALO:POOL_HALO + ts, cols]
        win = u
        for j in range(1, w):
            win = win + hist_ref[POOL_HALO - j:POOL_HALO - j + ts, cols]
        inv_count = 1.0 / jnp.minimum(pos + 1, w).astype(F32)
        pooled = win * inv_count - u
        mixed = _dot(pooled.astype(BF16), pool_w_ref[i]) * pool_s_ref[i:i + 1, :]
        cat_ref[:, cols] = mixed.astype(BF16)

    mem_out = _memory_attention(proj[:, tok_w:].astype(BF16), kb_ref[0], vb_ref[0], m_len)
    cat_ref[:, tok_w:] = mem_out.astype(BF16)

    mix = _dot(cat_ref[...], w_out_ref[...])
    o_ref[0] = _layer_norm(DN_ALPHA * x + mix, g_ref[...], b_ref[...])


def _mix_a(x, w_in_pad, pool_w_pad, pool_s_pad, k_big, v_big, w_out_pad, g, b, ts):
    B, S, D = x.shape
    wp = w_in_pad.shape[1]
    tok_w = N_POOL * POOL_PAD
    mm = k_big.shape[1]
    return pl.pallas_call(
        _mix_a_kernel,
        grid=(B, S // ts),
        in_specs=[pl.BlockSpec((1, ts, D), lambda bi, si: (bi, si, 0)),
                  _const_spec((D, wp)),
                  _const_spec((N_POOL, POOL_PAD, POOL_PAD)),
                  _const_spec((N_POOL, POOL_PAD)),
                  pl.BlockSpec((1, mm, MEM_WIDTH), lambda bi, si: (bi, 0, 0)),
                  pl.BlockSpec((1, mm, MEM_WIDTH), lambda bi, si: (bi, 0, 0)),
                  _const_spec((wp, D)),
                  _const_spec((1, D)),
                  _const_spec((1, D))],
        out_specs=pl.BlockSpec((1, ts, D), lambda bi, si: (bi, si, 0)),
        out_shape=jax.ShapeDtypeStruct((B, S, D), F32),
        scratch_shapes=[pltpu.VMEM((POOL_HALO + ts, tok_w), F32),
                        pltpu.VMEM((ts, wp), BF16)],
        compiler_params=_params(2),
        name="mix_a",
    )(x, w_in_pad, pool_w_pad, pool_s_pad, k_big, v_big, w_out_pad, g, b)


def _ffn_kernel(x_ref, wu_ref, wg_ref, cp_ref, wd_ref, g_ref, b_ref, o_ref,
                hu_ref, hg_ref, carry_ref, acc_ref):
    ts = x_ref.shape[1]
    n_chunks = wu_ref.shape[0]
    halo = SUBLANES
    s = pl.program_id(1)
    x = x_ref[0]
    xb = x.astype(BF16)

    @pl.when(s == 0)
    def _():
        carry_ref[...] = jnp.zeros(carry_ref.shape, F32)

    acc_ref[...] = jnp.zeros(acc_ref.shape, F32)

    def conv(h_ref, taps, bias):
        return (h_ref[halo - 2:halo - 2 + ts, :] * taps[0:1]
                + h_ref[halo - 1:halo - 1 + ts, :] * taps[1:2]
                + h_ref[halo:halo + ts, :] * taps[2:3] + bias)

    def chunk(c, carry):
        hu_ref[0:halo, :] = carry_ref[0, c]
        hg_ref[0:halo, :] = carry_ref[1, c]
        hu_ref[halo:halo + ts, :] = _dot(xb, wu_ref[c])
        hg_ref[halo:halo + ts, :] = _dot(xb, wg_ref[c])
        carry_ref[0, c] = hu_ref[ts:ts + halo, :]
        carry_ref[1, c] = hg_ref[ts:ts + halo, :]
        cp = cp_ref[c]
        u = conv(hu_ref, cp[0:3], cp[6:7])
        gate = conv(hg_ref, cp[3:6], cp[7:8])
        act = gate * (1.0 / (1.0 + jnp.exp(-gate))) * u
        acc_ref[...] += _dot(act.astype(BF16), wd_ref[c])
        return carry

    lax.fori_loop(0, n_chunks, chunk, 0)
    o_ref[0] = _layer_norm(DN_ALPHA * x + acc_ref[...], g_ref[...], b_ref[...])


def _ffn(x, wu3, wg3, cp3, wd3, g, b, ts):
    B, S, D = x.shape
    n_chunks, _, fc = wu3.shape
    return pl.pallas_call(
        _ffn_kernel,
        grid=(B, S // ts),
        in_specs=[pl.BlockSpec((1, ts, D), lambda bi, si: (bi, si, 0)),
                  _const_spec(wu3.shape),
                  _const_spec(wg3.shape),
                  _const_spec(cp3.shape),
                  _const_spec(wd3.shape),
                  _const_spec((1, D)),
                  _const_spec((1, D))],
        out_specs=pl.BlockSpec((1, ts, D), lambda bi, si: (bi, si, 0)),
        out_shape=jax.ShapeDtypeStruct((B, S, D), F32),
        scratch_shapes=[pltpu.VMEM((SUBLANES + ts, fc), F32),
                        pltpu.VMEM((SUBLANES + ts, fc), F32),
                        pltpu.VMEM((2, n_chunks, SUBLANES, fc), F32),
                        pltpu.VMEM((ts, D), F32)],
        compiler_params=_params(2),
        name="ffn",
    )(x, wu3, wg3, cp3, wd3, g, b)


def _qkv_kernel(x_ref, w_nat_ref, w_tr_ref, fb_ref, tri_ref, pqt_ref, pk_ref, cq_ref, ck_ref,
                qt_ref, qm_ref, k_ref, vt_ref, fqt_ref, fk_ref, carry_ref):
    ts = x_ref.shape[1]
    tok_w = k_ref.shape[2]
    s = pl.program_id(1)
    xb = x_ref[0].astype(BF16)
    nat = _dot(xb, w_nat_ref[...])
    qm_ref[0] = nat[:, :MEM_WIDTH].astype(BF16)
    k_ref[0] = nat[:, MEM_WIDTH:].astype(BF16)
    tr = _dot_nt(w_tr_ref[...], xb)
    qt_ref[0, 0] = tr[0:tok_w].astype(BF16)
    vt_ref[0, 0] = tr[tok_w:2 * tok_w].astype(BF16)

    @pl.when(s == 0)
    def _():
        carry_ref[...] = jnp.zeros(carry_ref.shape, F32)

    z = tr[2 * tok_w:] + fb_ref[...]
    log_f = jnp.minimum(z, 0.0) - jnp.log1p(jnp.exp(-jnp.abs(z)))
    part = _dot(jnp.concatenate(_split3(log_f), axis=0), tri_ref[...])
    cum_t = (part[0:LANES] + part[LANES:2 * LANES] + part[2 * LANES:]) + carry_ref[:, 0:1]
    carry_ref[:, 0:1] = cum_t[:, ts - 1:ts]
    pieces_t = jnp.concatenate(_split3(cum_t), axis=0)
    fqt_ref[0, 0] = (_dot(pqt_ref[...], pieces_t) + cq_ref[...]).astype(BF16)
    pieces = jnp.concatenate(_split3(cum_t.T), axis=1)
    fk_ref[0] = (_dot(pieces, pk_ref[...]) + ck_ref[...]).astype(BF16)


def _qkv(x, w_nat, w_tr, fb_col, tri, pqt, pk, cq_col, ck, tok_w, ts):
    B, S, D = x.shape
    n_tiles = S // ts
    row = lambda width: pl.BlockSpec((1, ts, width), lambda bi, si: (bi, si, 0))
    row_shape = lambda width: jax.ShapeDtypeStruct((B, S, width), BF16)
    col = lambda height: pl.BlockSpec((1, 1, height, ts), lambda bi, si: (bi, si, 0, 0))
    col_shape = lambda height: jax.ShapeDtypeStruct((B, n_tiles, height, ts), BF16)
    return pl.pallas_call(
        _qkv_kernel,
        grid=(B, n_tiles),
        in_specs=[pl.BlockSpec((1, ts, D), lambda bi, si: (bi, si, 0)),
                  _const_spec(w_nat.shape), _const_spec(w_tr.shape), _const_spec(fb_col.shape),
                  _const_spec(tri.shape), _const_spec(pqt.shape), _const_spec(pk.shape),
                  _const_spec(cq_col.shape), _const_spec(ck.shape)],
        out_specs=[col(tok_w), row(MEM_WIDTH), row(tok_w), col(tok_w), col(LANES), row(LANES)],
        out_shape=[col_shape(tok_w), row_shape(MEM_WIDTH), row_shape(tok_w), col_shape(tok_w),
                   col_shape(LANES), row_shape(LANES)],
        scratch_shapes=[pltpu.VMEM((LANES, LANES), F32)],
        compiler_params=_params(2),
        name="qkv",
    )(x, w_nat, w_tr, fb_col, tri, pqt, pk, cq_col, ck)


def _fox_kernel(qt_ref, fqt_ref, k_ref, fk_ref, vt_ref, o_ref,
                kaug_ref, qat_ref, m_ref, l_ref, acc_ref):
    n_sub, tk = qt_ref.shape[1], qt_ref.shape[3]
    tq = n_sub * tk
    seq = k_ref.shape[1]
    j = pl.program_id(1)
    i = pl.program_id(2)

    @pl.when(i == 0)
    def _():
        lane = lax.broadcasted_iota(jnp.int32, (seq, LANES), 1)
        kp = k_ref[0]
        fk = fk_ref[0]
        zero = jnp.zeros((seq, LANES), BF16)
        for h in range(2):
            head = 2 * j + h
            kaug_ref[h, :, 0:LANES] = jnp.where(lane // HEAD_DIM == h, kp, zero)
            in_head = (lane >= head * DECAY_LANES) & (lane < (head + 1) * DECAY_LANES)
            kaug_ref[h, :, LANES:] = jnp.where(in_head, fk, zero)

    for c in range(n_sub):
        qat_ref[0:LANES, c * tk:(c + 1) * tk] = qt_ref[0, c]
        qat_ref[LANES:, c * tk:(c + 1) * tk] = fqt_ref[0, c]
    m_ref[...] = jnp.full(m_ref.shape, NEG_BIG, F32)
    l_ref[...] = jnp.zeros(l_ref.shape, F32)
    acc_ref[...] = jnp.zeros(acc_ref.shape, F32)

    def update(h, c0, t, masked):
        cols = slice(c0, tq)
        sc = _dot(kaug_ref[h, pl.ds(pl.multiple_of(t * tk, tk), tk), :], qat_ref[:, cols])
        if masked:
            key = lax.broadcasted_iota(jnp.int32, sc.shape, 0)
            qry = lax.broadcasted_iota(jnp.int32, sc.shape, 1)
            sc = jnp.where(qry >= key, sc, NEG_BIG)
        m_old = m_ref[h, :, cols]
        m_new = jnp.maximum(m_old, jnp.max(sc, axis=0, keepdims=True))
        a = jnp.exp(m_old - m_new)
        p = jnp.exp(sc - m_new)
        l_ref[h, :, cols] = a * l_ref[h, :, cols] + jnp.sum(p, axis=0, keepdims=True)
        acc_ref[h, :, cols] = a * acc_ref[h, :, cols] + _dot(vt_ref[0, t], p.astype(BF16))
        m_ref[h, :, cols] = m_new

    def below_diagonal(t, carry):
        for h in range(2):
            update(h, 0, t, False)
        return carry

    lax.fori_loop(0, n_sub * i, below_diagonal, 0)

    for c in range(n_sub):
        for h in range(2):
            update(h, c * tk, n_sub * i + c, True)

    feat = lax.broadcasted_iota(jnp.int32, (LANES, tq), 0)
    out_t = jnp.where(feat < HEAD_DIM, acc_ref[0] / l_ref[0], acc_ref[1] / l_ref[1])
    o_ref[0] = out_t.T.astype(BF16)


def _fox(qt, fqt, k, fk, vt, tq):
    B, n_tiles, tok_w, tk = qt.shape
    S = n_tiles * tk
    n_pairs = tok_w // LANES
    n_sub = tq // tk
    return pl.pallas_call(
        _fox_kernel,
        grid=(B, n_pairs, S // tq),
        in_specs=[pl.BlockSpec((1, n_sub, LANES, tk), lambda b, j, i: (b, i, j, 0)),
                  pl.BlockSpec((1, n_sub, LANES, tk), lambda b, j, i: (b, i, 0, 0)),
                  pl.BlockSpec((1, S, LANES), lambda b, j, i: (b, 0, j)),
                  pl.BlockSpec((1, S, LANES), lambda b, j, i: (b, 0, 0)),
                  pl.BlockSpec((1, n_tiles, LANES, tk), lambda b, j, i: (b, 0, j, 0))],
        out_specs=pl.BlockSpec((1, tq, LANES), lambda b, j, i: (b, i, j)),
        out_shape=jax.ShapeDtypeStruct((B, S, tok_w), BF16),
        scratch_shapes=[pltpu.VMEM((2, S, 2 * LANES), BF16),
                        pltpu.VMEM((2 * LANES, tq), BF16),
                        pltpu.VMEM((2, 1, tq), F32),
                        pltpu.VMEM((2, 1, tq), F32),
                        pltpu.VMEM((2, LANES, tq), F32)],
        compiler_params=_params(3),
        name="fox",
    )(qt, fqt, k, fk, vt)


def _mix_b_kernel(x_ref, tok_ref, qm_ref, kb_ref, vb_ref, w_tok_ref, w_mem_ref, g_ref, b_ref, o_ref):
    m_len = kb_ref.shape[1] // MEM_HEADS
    mem_out = _memory_attention(qm_ref[0], kb_ref[0], vb_ref[0], m_len)
    mix = _dot(tok_ref[0], w_tok_ref[...]) + _dot(mem_out.astype(BF16), w_mem_ref[...])
    o_ref[0] = _layer_norm(DN_ALPHA * x_ref[0] + mix, g_ref[...], b_ref[...])


def _mix_b(x, tok, qm, k_big, v_big, w_tok, w_mem, g, b, ts):
    B, S, D = x.shape
    tok_w = tok.shape[2]
    mm = k_big.shape[1]
    return pl.pallas_call(
        _mix_b_kernel,
        grid=(B, S // ts),
        in_specs=[pl.BlockSpec((1, ts, D), lambda bi, si: (bi, si, 0)),
                  pl.BlockSpec((1, ts, tok_w), lambda bi, si: (bi, si, 0)),
                  pl.BlockSpec((1, ts, MEM_WIDTH), lambda bi, si: (bi, si, 0)),
                  pl.BlockSpec((1, mm, MEM_WIDTH), lambda bi, si: (bi, 0, 0)),
                  pl.BlockSpec((1, mm, MEM_WIDTH), lambda bi, si: (bi, 0, 0)),
                  _const_spec(w_tok.shape), _const_spec(w_mem.shape),
                  _const_spec((1, D)), _const_spec((1, D))],
        out_specs=pl.BlockSpec((1, ts, D), lambda bi, si: (bi, si, 0)),
        out_shape=jax.ShapeDtypeStruct((B, S, D), F32),
        compiler_params=_params(2),
        name="mix_b",
    )(x, tok, qm, k_big, v_big, w_tok, w_mem, g, b)


def _prep_ffn(w_up, conv_w, conv_b, w_down, fc):
    d, two_f = w_up.shape
    f = two_f // 2
    fp = _round_up(f, fc)
    n_chunks = fp // fc
    pad_cols = lambda a: jnp.pad(a, ((0, 0), (0, fp - f)))
    chunked = lambda a: a.reshape(a.shape[0], n_chunks, fc).transpose(1, 0, 2)
    wu3 = chunked(pad_cols(w_up[:, :f])).astype(BF16)
    wg3 = chunked(pad_cols(w_up[:, f:])).astype(BF16)
    rows = jnp.concatenate([conv_w[:, :f], conv_w[:, f:], conv_b[None, :f], conv_b[None, f:]], axis=0)
    cp3 = chunked(pad_cols(rows))
    wd3 = jnp.pad(w_down, ((0, fp - f), (0, 0))).reshape(n_chunks, fc, d).astype(BF16)
    return wu3, wg3, cp3, wd3


def _prep_mix_a(w_in, pool_w, pool_scale, w_out):
    d = w_in.shape[0]
    grp = pool_w.shape[1]
    tok_w = N_POOL * grp
    pad = POOL_PAD - grp
    w_tok = jnp.pad(w_in[:, :tok_w].reshape(d, N_POOL, grp), ((0, 0), (0, 0), (0, pad)))
    w_in_pad = jnp.concatenate([w_tok.reshape(d, N_POOL * POOL_PAD), w_in[:, tok_w:]], axis=1)
    pool_w_pad = jnp.pad(pool_w, ((0, 0), (0, pad), (0, pad)))
    pool_s_pad = jnp.pad(pool_scale.reshape(N_POOL, grp), ((0, 0), (0, pad)))
    o_tok = jnp.pad(w_out[:tok_w].reshape(N_POOL, grp, d), ((0, 0), (0, pad), (0, 0)))
    w_out_pad = jnp.concatenate([o_tok.reshape(N_POOL * POOL_PAD, d), w_out[tok_w:]], axis=0)
    return w_in_pad.astype(BF16), pool_w_pad.astype(BF16), pool_s_pad, w_out_pad.astype(BF16)


def _prep_qkv(w_q, kv_w, f_b, tok_w, ts):
    n_heads = tok_w // HEAD_DIM
    assert n_heads * DECAY_LANES <= LANES
    w_f = jnp.pad(kv_w[:, 2 * tok_w:], ((0, 0), (0, LANES - n_heads)))
    w_nat = jnp.concatenate([w_q[:, tok_w:], kv_w[:, :tok_w]], axis=1).astype(BF16)
    w_tr = jnp.concatenate([w_q[:, :tok_w] * ATTN_SCALE, kv_w[:, tok_w:2 * tok_w], w_f],
                           axis=1).T.astype(BF16)
    fb_col = jnp.pad(f_b, (0, LANES - n_heads)).reshape(LANES, 1)
    tri = (jnp.arange(ts)[:, None] <= jnp.arange(ts)[None, :]).astype(BF16)
    src = jnp.arange(DECAY_PARTS * LANES)
    part, head = src // LANES, src % LANES
    dst = jnp.arange(LANES)
    valid = (head < n_heads)[:, None]
    pq = (valid & (dst[None, :] == (head * DECAY_LANES + part)[:, None])).astype(BF16)
    pk = -(valid & (dst[None, :] == (head * DECAY_LANES + DECAY_PARTS + part)[:, None])).astype(BF16)
    in_use = dst < n_heads * DECAY_LANES
    cq_col = (in_use & (dst % DECAY_LANES >= DECAY_PARTS)).astype(F32).reshape(LANES, 1)
    ck = (in_use & (dst % DECAY_LANES < DECAY_PARTS)).astype(F32).reshape(1, LANES)
    return w_nat, w_tr, fb_col, tri, pq.T, pk, cq_col, ck


ROW_TILE = 512
FFN_CHUNK = 256
ATTN_TILE = 1024


def kernel(x, mem, a_w_in, a_pool_w, a_pool_scale, a_w_out, b_w_q, b_w_out, kv_w, f_b, mem_w_kv,
           ln1_g, ln1_b, ln2_g, ln2_b, ffn_w_up, ffn_conv_w, ffn_conv_b, ffn_w_down):
    B, S, D = x.shape
    n_a = a_w_in.shape[0]
    n_b = b_w_q.shape[0]
    tok_w = D - MEM_WIDTH
    ts = min(ROW_TILE, S)
    tq = min(ATTN_TILE, S)
    assert S % ts == 0 and S % tq == 0 and tq % ts == 0 and tok_w % LANES == 0

    k_big, v_big = _memkv(mem, mem_w_kv.astype(BF16))
    row = lambda a, l: a[l].reshape(1, D)

    shared = None
    for l in range(n_a + n_b):
        if l < n_a:
            w_in_pad, pool_w_pad, pool_s_pad, w_out_pad = _prep_mix_a(
                a_w_in[l], a_pool_w[l], a_pool_scale[l], a_w_out[l])
            x = _mix_a(x, w_in_pad, pool_w_pad, pool_s_pad, k_big[l], v_big[l], w_out_pad,
                       row(ln1_g, l), row(ln1_b, l), ts)
        else:
            j = l - n_a
            qt, qm, k_new, vt_new, fqt_new, fk_new = _qkv(
                x, *_prep_qkv(b_w_q[j], kv_w, f_b, tok_w, ts), tok_w, ts)
            if j == 0:
                shared = (fqt_new, k_new, fk_new, vt_new)
            fqt, k_sh, fk_sh, vt_sh = shared
            tok = _fox(qt, fqt, k_sh, fk_sh, vt_sh, tq)
            w_out = b_w_out[j].astype(BF16)
            x = _mix_b(x, tok, qm, k_big[l], v_big[l], w_out[:tok_w], w_out[tok_w:],
                       row(ln1_g, l), row(ln1_b, l), ts)
        wu3, wg3, cp3, wd3 = _prep_ffn(ffn_w_up[l], ffn_conv_w[l], ffn_conv_b[l], ffn_w_down[l], FFN_CHUNK)
        x = _ffn(x, wu3, wg3, cp3, wd3, row(ln2_g, l), row(ln2_b, l), ts)
    return x
```

```python
import functools

import jax
import jax.numpy as jnp
from jax import lax
from jax.experimental import pallas as pl
from jax.experimental.pallas import tpu as pltpu

F32 = jnp.float32
BF16 = jnp.bfloat16

HEAD_DIM = 64
MEM_HEADS = 4
MEM_WIDTH = MEM_HEADS * HEAD_DIM
POOL_WINDOWS = (2, 4, 8, 16)
N_POOL = len(POOL_WINDOWS)
CONV_WIDTH = 3
LN_EPS = 1e-5
DEPTH = 2
DN_ALPHA = (2.0 * DEPTH) ** 0.25
ATTN_SCALE = HEAD_DIM ** -0.5
LOG2E = 1.4426950408889634

LANES = 128
SUBLANES = 8
MXU_N = 256
VMEM_LIMIT_BYTES = 56 * 1024 * 1024

POOL_PAD = MXU_N
POOL_HALO = 16
DECAY_PARTS = 3
DECAY_LANES = 2 * DECAY_PARTS
NEG_BIG = -1e30


def _round_up(n, m):
    return (n + m - 1) // m * m


def _const_spec(shape):
    nd = len(shape)
    return pl.BlockSpec(shape, lambda *_: (0,) * nd, pipeline_mode=pl.Buffered(1))


def _params(n_grid):
    return pltpu.CompilerParams(dimension_semantics=("arbitrary",) * n_grid,
                                vmem_limit_bytes=VMEM_LIMIT_BYTES)


def _layer_norm(y, g, b):
    mu = jnp.mean(y, axis=-1, keepdims=True)
    d = y - mu
    var = jnp.mean(d * d, axis=-1, keepdims=True)
    return d * lax.rsqrt(var + LN_EPS) * g + b


def _dot(a, b):
    return jnp.dot(a, b, preferred_element_type=F32)


def _dot_nt(a, b):
    return lax.dot_general(a, b, (((1,), (1,)), ((), ())), preferred_element_type=F32)


def _split3(x):
    hi = x.astype(BF16)
    r = x - hi.astype(F32)
    mid = r.astype(BF16)
    lo = (r - mid.astype(F32)).astype(BF16)
    return hi, mid, lo


def _memkv_kernel(mem_ref, w_ref, k_ref, v_ref):
    m_len = mem_ref.shape[1]
    kv = _dot(mem_ref[0].astype(BF16), w_ref[0])
    k = kv[:, :MEM_WIDTH] * ATTN_SCALE
    v = kv[:, MEM_WIDTH:]
    head_of_lane = lax.broadcasted_iota(jnp.int32, (m_len, MEM_WIDTH), 1) // HEAD_DIM
    for h in range(MEM_HEADS):
        sel = head_of_lane == h
        k_ref[0, 0, h * m_len:(h + 1) * m_len, :] = jnp.where(sel, k, 0.0).astype(BF16)
        v_ref[0, 0, h * m_len:(h + 1) * m_len, :] = jnp.where(sel, v, 0.0).astype(BF16)


def _memkv(mem, w_kv_bf16):
    B, M, D = mem.shape
    L = w_kv_bf16.shape[0]
    out = jax.ShapeDtypeStruct((L, B, MEM_HEADS * M, MEM_WIDTH), BF16)
    out_spec = pl.BlockSpec((1, 1, MEM_HEADS * M, MEM_WIDTH), lambda l, b: (l, b, 0, 0))
    return pl.pallas_call(
        _memkv_kernel,
        grid=(L, B),
        in_specs=[pl.BlockSpec((1, M, D), lambda l, b: (b, 0, 0)),
                  pl.BlockSpec((1, D, 2 * MEM_WIDTH), lambda l, b: (l, 0, 0))],
        out_specs=[out_spec, out_spec],
        out_shape=[out, out],
        compiler_params=_params(2),
        name="memkv",
    )(mem, w_kv_bf16)


def _memory_attention(q_mem_bf16, k_big, v_big, m_len):
    logits = _dot_nt(q_mem_bf16, k_big)
    probs = []
    for h in range(MEM_HEADS):
        lh = logits[:, h * m_len:(h + 1) * m_len]
        e = jnp.exp(lh - jnp.max(lh, axis=-1, keepdims=True))
        p = e / jnp.sum(e, axis=-1, keepdims=True)
        probs.append(p.astype(BF16))
    return _dot(jnp.concatenate(probs, axis=1), v_big)


def _mix_a_kernel(x_ref, w_in_ref, pool_w_ref, pool_s_ref, kb_ref, vb_ref, w_out_ref, g_ref, b_ref,
                  o_ref, hist_ref, cat_ref):
    ts = x_ref.shape[1]
    m_len = kb_ref.shape[1] // MEM_HEADS
    tok_w = N_POOL * POOL_PAD
    s = pl.program_id(1)
    x = x_ref[0]
    proj = _dot(x.astype(BF16), w_in_ref[...])

    @pl.when(s == 0)
    def _():
        hist_ref[0:POOL_HALO, :] = jnp.zeros((POOL_HALO, tok_w), F32)

    @pl.when(s > 0)
    def _():
        hist_ref[0:POOL_HALO, :] = hist_ref[ts:ts + POOL_HALO, :]

    hist_ref[POOL_HALO:POOL_HALO + ts, :] = proj[:, :tok_w]

    pos = s * ts + lax.broadcasted_iota(jnp.int32, (ts, 1), 0)
    for i, w in enumerate(POOL_WINDOWS):
        cols = slice(i * POOL_PAD, (i + 1) * POOL_PAD)
        u = hist_ref[POOL_HALO:POOL_HALO + ts, cols]
        win = u
        for j in range(1, w):
            win = win + hist_ref[POOL_HALO - j:POOL_HALO - j + ts, cols]
        inv_count = 1.0 / jnp.minimum(pos + 1, w).astype(F32)
        pooled = win * inv_count - u
        mixed = _dot(pooled.astype(BF16), pool_w_ref[i]) * pool_s_ref[i:i + 1, :]
        cat_ref[:, cols] = mixed.astype(BF16)

    mem_out = _memory_attention(proj[:, tok_w:].astype(BF16), kb_ref[0], vb_ref[0], m_len)
    cat_ref[:, tok_w:] = mem_out.astype(BF16)

    mix = _dot(cat_ref[...], w_out_ref[...])
    o_ref[0] = _layer_norm(DN_ALPHA * x + mix, g_ref[...], b_ref[...])


def _mix_a(x, w_in_pad, pool_w_pad, pool_s_pad, k_big, v_big, w_out_pad, g, b, ts):
    B, S, D = x.shape
    wp = w_in_pad.shape[1]
    tok_w = N_POOL * POOL_PAD
    mm = k_big.shape[1]
    return pl.pallas_call(
        _mix_a_kernel,
        grid=(B, S // ts),
        in_specs=[pl.BlockSpec((1, ts, D), lambda bi, si: (bi, si, 0)),
                  _const_spec((D, wp)),
                  _const_spec((N_POOL, POOL_PAD, POOL_PAD)),
                  _const_spec((N_POOL, POOL_PAD)),
                  pl.BlockSpec((1, mm, MEM_WIDTH), lambda bi, si: (bi, 0, 0)),
                  pl.BlockSpec((1, mm, MEM_WIDTH), lambda bi, si: (bi, 0, 0)),
                  _const_spec((wp, D)),
                  _const_spec((1, D)),
                  _const_spec((1, D))],
        out_specs=pl.BlockSpec((1, ts, D), lambda bi, si: (bi, si, 0)),
        out_shape=jax.ShapeDtypeStruct((B, S, D), F32),
        scratch_shapes=[pltpu.VMEM((POOL_HALO + ts, tok_w), F32),
                        pltpu.VMEM((ts, wp), BF16)],
        compiler_params=_params(2),
        name="mix_a",
    )(x, w_in_pad, pool_w_pad, pool_s_pad, k_big, v_big, w_out_pad, g, b)


def _ffn_kernel(fc, x_ref, wu_ref, wg_ref, cp_ref, wd_ref, g_ref, b_ref, o_ref,
                hu_ref, hg_ref, carry_ref, act_ref):
    ts = x_ref.shape[1]
    n_chunks = wu_ref.shape[1] // fc
    halo = SUBLANES
    s = pl.program_id(1)
    x = x_ref[0]
    xb = x.astype(BF16)

    @pl.when(s == 0)
    def _():
        carry_ref[...] = jnp.zeros(carry_ref.shape, F32)

    def conv(h_ref, slot, taps, bias):
        return (h_ref[slot, halo - 2:halo - 2 + ts, :] * taps[0:1]
                + h_ref[slot, halo - 1:halo - 1 + ts, :] * taps[1:2]
                + h_ref[slot, halo:halo + ts, :] * taps[2:3] + bias)

    for c in range(n_chunks):
        cols = slice(c * fc, (c + 1) * fc)
        slot = c % 2
        hu_ref[slot, 0:halo, :] = carry_ref[0, :, cols]
        hg_ref[slot, 0:halo, :] = carry_ref[1, :, cols]
        hu_ref[slot, halo:halo + ts, :] = _dot(xb, wu_ref[:, cols])
        hg_ref[slot, halo:halo + ts, :] = _dot(xb, wg_ref[:, cols])
        carry_ref[0, :, cols] = hu_ref[slot, ts:ts + halo, :]
        carry_ref[1, :, cols] = hg_ref[slot, ts:ts + halo, :]
        cp = cp_ref[:, cols]
        u = conv(hu_ref, slot, cp[0:3], cp[6:7])
        gate = conv(hg_ref, slot, cp[3:6], cp[7:8])
        act_ref[:, cols] = (gate * (1.0 / (1.0 + jnp.exp(-gate))) * u).astype(BF16)

    ffn = _dot(act_ref[...], wd_ref[...])
    o_ref[0] = _layer_norm(DN_ALPHA * x + ffn, g_ref[...], b_ref[...])


def _ffn(x, wu, wg, cp, wd, g, b, ts, fc):
    B, S, D = x.shape
    fp = wu.shape[1]
    return pl.pallas_call(
        functools.partial(_ffn_kernel, fc),
        grid=(B, S // ts),
        in_specs=[pl.BlockSpec((1, ts, D), lambda bi, si: (bi, si, 0)),
                  _const_spec(wu.shape),
                  _const_spec(wg.shape),
                  _const_spec(cp.shape),
                  _const_spec(wd.shape),
                  _const_spec((1, D)),
                  _const_spec((1, D))],
        out_specs=pl.BlockSpec((1, ts, D), lambda bi, si: (bi, si, 0)),
        out_shape=jax.ShapeDtypeStruct((B, S, D), F32),
        scratch_shapes=[pltpu.VMEM((2, SUBLANES + ts, fc), F32),
                        pltpu.VMEM((2, SUBLANES + ts, fc), F32),
                        pltpu.VMEM((2, SUBLANES, fp), F32),
                        pltpu.VMEM((ts, fp), BF16)],
        compiler_params=_params(2),
        name="ffn",
    )(x, wu, wg, cp, wd, g, b)


def _qkv_kernel(x_ref, w_nat_ref, w_tr_ref, fb_ref, tri_ref, pqt_ref, pk_ref, cq_ref, ck_ref,
                qt_ref, qm_ref, k_ref, vt_ref, fqt_ref, fk_ref, carry_ref):
    ts = x_ref.shape[1]
    tok_w = k_ref.shape[2]
    s = pl.program_id(1)
    xb = x_ref[0].astype(BF16)
    nat = _dot(xb, w_nat_ref[...])
    qm_ref[0] = nat[:, :MEM_WIDTH].astype(BF16)
    k_ref[0] = nat[:, MEM_WIDTH:].astype(BF16)
    tr = _dot_nt(w_tr_ref[...], xb)
    qt_ref[0, 0] = tr[0:tok_w].astype(BF16)
    vt_ref[0, 0] = tr[tok_w:2 * tok_w].astype(BF16)

    @pl.when(s == 0)
    def _():
        carry_ref[...] = jnp.zeros(carry_ref.shape, F32)

    z = tr[2 * tok_w:] + fb_ref[...]
    log_f = jnp.minimum(z, 0.0) - jnp.log1p(jnp.exp(-jnp.abs(z)))
    part = _dot(jnp.concatenate(_split3(log_f), axis=0), tri_ref[...])
    cum_t = (part[0:LANES] + part[LANES:2 * LANES] + part[2 * LANES:]) + carry_ref[:, 0:1]
    carry_ref[:, 0:1] = cum_t[:, ts - 1:ts]
    dec_t = cum_t * LOG2E
    pieces_t = jnp.concatenate(_split3(dec_t), axis=0)
    fqt_ref[0, 0] = (_dot(pqt_ref[...], pieces_t) + cq_ref[...]).astype(BF16)
    pieces = jnp.concatenate(_split3(dec_t.T), axis=1)
    fk_ref[0] = (_dot(pieces, pk_ref[...]) + ck_ref[...]).astype(BF16)


def _qkv(x, w_nat, w_tr, fb_col, tri, pqt, pk, cq_col, ck, tok_w, ts):
    B, S, D = x.shape
    n_tiles = S // ts
    row = lambda width: pl.BlockSpec((1, ts, width), lambda bi, si: (bi, si, 0))
    row_shape = lambda width: jax.ShapeDtypeStruct((B, S, width), BF16)
    col = lambda height: pl.BlockSpec((1, 1, height, ts), lambda bi, si: (bi, si, 0, 0))
    col_shape = lambda height: jax.ShapeDtypeStruct((B, n_tiles, height, ts), BF16)
    return pl.pallas_call(
        _qkv_kernel,
        grid=(B, n_tiles),
        in_specs=[pl.BlockSpec((1, ts, D), lambda bi, si: (bi, si, 0)),
                  _const_spec(w_nat.shape), _const_spec(w_tr.shape), _const_spec(fb_col.shape),
                  _const_spec(tri.shape), _const_spec(pqt.shape), _const_spec(pk.shape),
                  _const_spec(cq_col.shape), _const_spec(ck.shape)],
        out_specs=[col(tok_w), row(MEM_WIDTH), row(tok_w), col(tok_w), col(LANES), row(LANES)],
        out_shape=[col_shape(tok_w), row_shape(MEM_WIDTH), row_shape(tok_w), col_shape(tok_w),
                   col_shape(LANES), row_shape(LANES)],
        scratch_shapes=[pltpu.VMEM((LANES, LANES), F32)],
        compiler_params=_params(2),
        name="qkv",
    )(x, w_nat, w_tr, fb_col, tri, pqt, pk, cq_col, ck)


def _fox_kernel(qt_ref, fqt_ref, k_ref, fk_ref, vt_ref, o_ref,
                kaug_ref, vaug_ref, qat_ref, sc_ref, m_ref, acc_ref):
    n_sub, tk = qt_ref.shape[1], qt_ref.shape[3]
    tq = n_sub * tk
    seq = k_ref.shape[1]
    j = pl.program_id(1)
    i = pl.program_id(2)

    @pl.when(i == 0)
    def _():
        lane = lax.broadcasted_iota(jnp.int32, (seq, LANES), 1)
        kp = k_ref[0]
        fk = fk_ref[0]
        zero = jnp.zeros((seq, LANES), BF16)
        vt = vt_ref[0]
        feat = lax.broadcasted_iota(jnp.int32, vt.shape, 1)
        one = jnp.ones(vt.shape, BF16)
        for h in range(2):
            head = 2 * j + h
            kaug_ref[h, :, 0:LANES] = jnp.where(lane // HEAD_DIM == h, kp, zero)
            in_head = (lane >= head * DECAY_LANES) & (lane < (head + 1) * DECAY_LANES)
            kaug_ref[h, :, LANES:] = jnp.where(in_head, fk, zero)
            vaug_ref[h] = jnp.where(feat // HEAD_DIM == h, vt, one)

    for c in range(n_sub):
        qat_ref[0:LANES, c * tk:(c + 1) * tk] = qt_ref[0, c]
        qat_ref[LANES:, c * tk:(c + 1) * tk] = fqt_ref[0, c]
    m_ref[...] = jnp.full(m_ref.shape, NEG_BIG, F32)
    acc_ref[...] = jnp.zeros(acc_ref.shape, F32)

    cw = sc_ref.shape[1]
    n_chunks = tq // cw

    def scores(h, cc, t):
        return _dot(kaug_ref[h, pl.ds(pl.multiple_of(t * tk, tk), tk), :],
                    qat_ref[:, cc * cw:(cc + 1) * cw])

    def fold(h, cc, t, sc, diag_offset):
        cols = slice(cc * cw, (cc + 1) * cw)
        if diag_offset is not None:
            key = lax.broadcasted_iota(jnp.int32, sc.shape, 0)
            qry = lax.broadcasted_iota(jnp.int32, sc.shape, 1)
            sc = jnp.where(qry + diag_offset >= key, sc, NEG_BIG)
        m_old = m_ref[h, :, cols]
        m_new = jnp.maximum(m_old, jnp.max(sc, axis=0, keepdims=True))
        a = jnp.exp2(m_old - m_new)
        p = jnp.exp2(sc - m_new).astype(BF16)
        acc_ref[h, :, cols] = a * acc_ref[h, :, cols] + _dot(vaug_ref[h, t], p)
        m_ref[h, :, cols] = m_new

    def run(steps, sc, after):
        for n, (h, cc, t, off) in enumerate(steps):
            nxt = steps[n + 1] if n + 1 < len(steps) else after
            sc_next = scores(*nxt[:3]) if nxt is not None else None
            fold(h, cc, t, sc, off)
            sc = sc_next
        return sc

    sc_ref[...] = scores(0, 0, 0)

    def below_diagonal(u, carry):
        tiles = [n_sub * u + c for c in range(n_sub)]
        steps = [(h, cc, t, None) for t in tiles for h in range(2) for cc in range(n_chunks)]
        sc_ref[...] = run(steps, sc_ref[...], (0, 0, n_sub * (u + 1)))
        return carry

    first = n_sub * i
    lax.fori_loop(0, i, below_diagonal, 0)

    steps = []
    for c in range(n_sub):
        for h in range(2):
            for cc in range(c * tk // cw, n_chunks):
                off = cc * cw - c * tk
                steps.append((h, cc, first + c, off if off < tk else None))
    run(steps, sc_ref[...], None)

    feat = lax.broadcasted_iota(jnp.int32, (LANES, tq), 0)
    acc0, acc1 = acc_ref[0], acc_ref[1]
    out_t = jnp.where(feat < HEAD_DIM, acc0 / acc0[HEAD_DIM:HEAD_DIM + 1], acc1 / acc1[0:1])
    o_ref[0] = out_t.T.astype(BF16)


def _fox(qt, fqt, k, fk, vt, tq):
    B, n_tiles, tok_w, tk = qt.shape
    S = n_tiles * tk
    n_pairs = tok_w // LANES
    n_sub = tq // tk
    return pl.pallas_call(
        _fox_kernel,
        grid=(B, n_pairs, S // tq),
        in_specs=[pl.BlockSpec((1, n_sub, LANES, tk), lambda b, j, i: (b, i, j, 0)),
                  pl.BlockSpec((1, n_sub, LANES, tk), lambda b, j, i: (b, i, 0, 0)),
                  pl.BlockSpec((1, S, LANES), lambda b, j, i: (b, 0, j)),
                  pl.BlockSpec((1, S, LANES), lambda b, j, i: (b, 0, 0)),
                  pl.BlockSpec((1, n_tiles, LANES, tk), lambda b, j, i: (b, 0, j, 0))],
        out_specs=pl.BlockSpec((1, tq, LANES), lambda b, j, i: (b, i, j)),
        out_shape=jax.ShapeDtypeStruct((B, S, tok_w), BF16),
        scratch_shapes=[pltpu.VMEM((2, S, 2 * LANES), BF16),
                        pltpu.VMEM((2, n_tiles, LANES, tk), BF16),
                        pltpu.VMEM((2 * LANES, tq), BF16),
                        pltpu.VMEM((tk, 2 * MXU_N), F32),
                        pltpu.VMEM((2, 1, tq), F32),
                        pltpu.VMEM((2, LANES, tq), F32)],
        compiler_params=_params(3),
        name="fox",
    )(qt, fqt, k, fk, vt)


def _mix_b_kernel(x_ref, tok_ref, qm_ref, kb_ref, vb_ref, w_tok_ref, w_mem_ref, g_ref, b_ref, o_ref):
    m_len = kb_ref.shape[1] // MEM_HEADS
    mem_out = _memory_attention(qm_ref[0], kb_ref[0], vb_ref[0], m_len)
    mix = _dot(tok_ref[0], w_tok_ref[...]) + _dot(mem_out.astype(BF16), w_mem_ref[...])
    o_ref[0] = _layer_norm(DN_ALPHA * x_ref[0] + mix, g_ref[...], b_ref[...])


def _mix_b(x, tok, qm, k_big, v_big, w_tok, w_mem, g, b, ts):
    B, S, D = x.shape
    tok_w = tok.shape[2]
    mm = k_big.shape[1]
    return pl.pallas_call(
        _mix_b_kernel,
        grid=(B, S // ts),
        in_specs=[pl.BlockSpec((1, ts, D), lambda bi, si: (bi, si, 0)),
                  pl.BlockSpec((1, ts, tok_w), lambda bi, si: (bi, si, 0)),
                  pl.BlockSpec((1, ts, MEM_WIDTH), lambda bi, si: (bi, si, 0)),
                  pl.BlockSpec((1, mm, MEM_WIDTH), lambda bi, si: (bi, 0, 0)),
                  pl.BlockSpec((1, mm, MEM_WIDTH), lambda bi, si: (bi, 0, 0)),
                  _const_spec(w_tok.shape), _const_spec(w_mem.shape),
                  _const_spec((1, D)), _const_spec((1, D))],
        out_specs=pl.BlockSpec((1, ts, D), lambda bi, si: (bi, si, 0)),
        out_shape=jax.ShapeDtypeStruct((B, S, D), F32),
        compiler_params=_params(2),
        name="mix_b",
    )(x, tok, qm, k_big, v_big, w_tok, w_mem, g, b)


def _prep_ffn(w_up, conv_w, conv_b, w_down, fc):
    f = w_up.shape[1] // 2
    fp = _round_up(f, fc)
    pad_cols = lambda a: jnp.pad(a, ((0, 0), (0, fp - f)))
    wu = pad_cols(w_up[:, :f]).astype(BF16)
    wg = pad_cols(w_up[:, f:]).astype(BF16)
    cp = pad_cols(jnp.concatenate([conv_w[:, :f], conv_w[:, f:], conv_b[None, :f], conv_b[None, f:]],
                                  axis=0))
    wd = jnp.pad(w_down, ((0, fp - f), (0, 0))).astype(BF16)
    return wu, wg, cp, wd


def _prep_mix_a(w_in, pool_w, pool_scale, w_out):
    d = w_in.shape[0]
    grp = pool_w.shape[1]
    tok_w = N_POOL * grp
    pad = POOL_PAD - grp
    w_tok = jnp.pad(w_in[:, :tok_w].reshape(d, N_POOL, grp), ((0, 0), (0, 0), (0, pad)))
    w_in_pad = jnp.concatenate([w_tok.reshape(d, N_POOL * POOL_PAD), w_in[:, tok_w:]], axis=1)
    pool_w_pad = jnp.pad(pool_w, ((0, 0), (0, pad), (0, pad)))
    pool_s_pad = jnp.pad(pool_scale.reshape(N_POOL, grp), ((0, 0), (0, pad)))
    o_tok = jnp.pad(w_out[:tok_w].reshape(N_POOL, grp, d), ((0, 0), (0, pad), (0, 0)))
    w_out_pad = jnp.concatenate([o_tok.reshape(N_POOL * POOL_PAD, d), w_out[tok_w:]], axis=0)
    return w_in_pad.astype(BF16), pool_w_pad.astype(BF16), pool_s_pad, w_out_pad.astype(BF16)


def _prep_qkv(w_q, kv_w, f_b, tok_w, ts):
    n_heads = tok_w // HEAD_DIM
    assert n_heads * DECAY_LANES <= LANES
    w_f = jnp.pad(kv_w[:, 2 * tok_w:], ((0, 0), (0, LANES - n_heads)))
    w_nat = jnp.concatenate([w_q[:, tok_w:], kv_w[:, :tok_w]], axis=1).astype(BF16)
    w_tr = jnp.concatenate([w_q[:, :tok_w] * (ATTN_SCALE * LOG2E), kv_w[:, tok_w:2 * tok_w], w_f],
                           axis=1).T.astype(BF16)
    fb_col = jnp.pad(f_b, (0, LANES - n_heads)).reshape(LANES, 1)
    tri = (jnp.arange(ts)[:, None] <= jnp.arange(ts)[None, :]).astype(BF16)
    src = jnp.arange(DECAY_PARTS * LANES)
    part, head = src // LANES, src % LANES
    dst = jnp.arange(LANES)
    valid = (head < n_heads)[:, None]
    pq = (valid & (dst[None, :] == (head * DECAY_LANES + part)[:, None])).astype(BF16)
    pk = -(valid & (dst[None, :] == (head * DECAY_LANES + DECAY_PARTS + part)[:, None])).astype(BF16)
    in_use = dst < n_heads * DECAY_LANES
    cq_col = (in_use & (dst % DECAY_LANES >= DECAY_PARTS)).astype(F32).reshape(LANES, 1)
    ck = (in_use & (dst % DECAY_LANES < DECAY_PARTS)).astype(F32).reshape(1, LANES)
    return w_nat, w_tr, fb_col, tri, pq.T, pk, cq_col, ck


ROW_TILE = 512
FFN_CHUNK = 256
ATTN_TILE = 1024


def kernel(x, mem, a_w_in, a_pool_w, a_pool_scale, a_w_out, b_w_q, b_w_out, kv_w, f_b, mem_w_kv,
           ln1_g, ln1_b, ln2_g, ln2_b, ffn_w_up, ffn_conv_w, ffn_conv_b, ffn_w_down):
    B, S, D = x.shape
    n_a = a_w_in.shape[0]
    n_b = b_w_q.shape[0]
    tok_w = D - MEM_WIDTH
    ts = min(ROW_TILE, S)
    tq = min(ATTN_TILE, S)
    assert S % ts == 0 and S % tq == 0 and tq % ts == 0 and tok_w % LANES == 0

    k_big, v_big = _memkv(mem, mem_w_kv.astype(BF16))
    row = lambda a, l: a[l].reshape(1, D)

    shared = None
    for l in range(n_a + n_b):
        if l < n_a:
            w_in_pad, pool_w_pad, pool_s_pad, w_out_pad = _prep_mix_a(
                a_w_in[l], a_pool_w[l], a_pool_scale[l], a_w_out[l])
            x = _mix_a(x, w_in_pad, pool_w_pad, pool_s_pad, k_big[l], v_big[l], w_out_pad,
                       row(ln1_g, l), row(ln1_b, l), ts)
        else:
            j = l - n_a
            qt, qm, k_new, vt_new, fqt_new, fk_new = _qkv(
                x, *_prep_qkv(b_w_q[j], kv_w, f_b, tok_w, ts), tok_w, ts)
            if j == 0:
                shared = (fqt_new, k_new, fk_new, vt_new)
            fqt, k_sh, fk_sh, vt_sh = shared
            tok = _fox(qt, fqt, k_sh, fk_sh, vt_sh, tq)
            w_out = b_w_out[j].astype(BF16)
            x = _mix_b(x, tok, qm, k_big[l], v_big[l], w_out[:tok_w], w_out[tok_w:],
                       row(ln1_g, l), row(ln1_b, l), ts)
        wu, wg, cp, wd = _prep_ffn(ffn_w_up[l], ffn_conv_w[l], ffn_conv_b[l], ffn_w_down[l], FFN_CHUNK)
        x = _ffn(x, wu, wg, cp, wd, row(ln2_g, l), row(ln2_b, l), ts, FFN_CHUNK)
    return x
```

```python
import functools

import jax
import jax.numpy as jnp
from jax import lax
from jax.experimental import pallas as pl
from jax.experimental.pallas import tpu as pltpu

F32 = jnp.float32
BF16 = jnp.bfloat16

HEAD_DIM = 64
MEM_HEADS = 4
MEM_WIDTH = MEM_HEADS * HEAD_DIM
POOL_WINDOWS = (2, 4, 8, 16)
N_POOL = len(POOL_WINDOWS)
CONV_WIDTH = 3
LN_EPS = 1e-5
DEPTH = 2
DN_ALPHA = (2.0 * DEPTH) ** 0.25
ATTN_SCALE = HEAD_DIM ** -0.5
LOG2E = 1.4426950408889634

LANES = 128
SUBLANES = 8
MXU_N = 256
VMEM_LIMIT_BYTES = 56 * 1024 * 1024

POOL_PAD = MXU_N
POOL_HALO = 24
DECAY_PARTS = 3
DECAY_LANES = 2 * DECAY_PARTS
NEG_BIG = -1e30


def _round_up(n, m):
    return (n + m - 1) // m * m


def _const_spec(shape):
    nd = len(shape)
    return pl.BlockSpec(shape, lambda *_: (0,) * nd, pipeline_mode=pl.Buffered(1))


def _params(n_grid):
    return pltpu.CompilerParams(dimension_semantics=("arbitrary",) * n_grid,
                                vmem_limit_bytes=VMEM_LIMIT_BYTES)


def _layer_norm(y, g, b):
    mu = jnp.mean(y, axis=-1, keepdims=True)
    d = y - mu
    var = jnp.mean(d * d, axis=-1, keepdims=True)
    return d * lax.rsqrt(var + LN_EPS) * g + b


def _dot(a, b):
    return jnp.dot(a, b, preferred_element_type=F32)


def _dot_nt(a, b):
    return lax.dot_general(a, b, (((1,), (1,)), ((), ())), preferred_element_type=F32)


def _split3(x):
    hi = x.astype(BF16)
    r = x - hi.astype(F32)
    mid = r.astype(BF16)
    lo = (r - mid.astype(F32)).astype(BF16)
    return hi, mid, lo


def _memkv_kernel(mem_ref, w_ref, k_ref, v_ref):
    m_len = mem_ref.shape[1]
    kv = _dot(mem_ref[0].astype(BF16), w_ref[0])
    k = kv[:, :MEM_WIDTH] * ATTN_SCALE
    v = kv[:, MEM_WIDTH:]
    head_of_lane = lax.broadcasted_iota(jnp.int32, (m_len, MEM_WIDTH), 1) // HEAD_DIM
    for h in range(MEM_HEADS):
        sel = head_of_lane == h
        k_ref[0, 0, h * m_len:(h + 1) * m_len, :] = jnp.where(sel, k, 0.0).astype(BF16)
        v_ref[0, 0, h * m_len:(h + 1) * m_len, :] = jnp.where(sel, v, 0.0).astype(BF16)


def _memkv(mem, w_kv_bf16):
    B, M, D = mem.shape
    L = w_kv_bf16.shape[0]
    out = jax.ShapeDtypeStruct((L, B, MEM_HEADS * M, MEM_WIDTH), BF16)
    out_spec = pl.BlockSpec((1, 1, MEM_HEADS * M, MEM_WIDTH), lambda l, b: (l, b, 0, 0))
    return pl.pallas_call(
        _memkv_kernel,
        grid=(L, B),
        in_specs=[pl.BlockSpec((1, M, D), lambda l, b: (b, 0, 0)),
                  pl.BlockSpec((1, D, 2 * MEM_WIDTH), lambda l, b: (l, 0, 0))],
        out_specs=[out_spec, out_spec],
        out_shape=[out, out],
        compiler_params=_params(2),
        name="memkv",
    )(mem, w_kv_bf16)


def _memory_attention(q_mem_bf16, k_big, v_big, head_ones, m_len):
    logits = _dot_nt(q_mem_bf16, k_big)
    weights = []
    for h in range(MEM_HEADS):
        lh = logits[:, h * m_len:(h + 1) * m_len]
        weights.append(jnp.exp(lh - jnp.max(lh, axis=-1, keepdims=True)).astype(BF16))
    e = jnp.concatenate(weights, axis=1)
    return _dot(e, v_big) / _dot(e, head_ones)


def _mix_a_kernel(x_ref, w_in_ref, pool_w_ref, pool_s_ref, kb_ref, vb_ref, ones_ref, w_out_ref, g_ref, b_ref,
                  o_ref, hist_ref, lvl_ref, cat_ref):
    ts = x_ref.shape[1]
    m_len = kb_ref.shape[1] // MEM_HEADS
    tok_w = N_POOL * POOL_PAD
    n_rows = POOL_HALO + ts
    s = pl.program_id(1)
    x = x_ref[0]
    proj = _dot(x.astype(BF16), w_in_ref[...])

    @pl.when(s == 0)
    def _():
        hist_ref[0:POOL_HALO, :] = jnp.zeros((POOL_HALO, tok_w), F32)
        lvl_ref[:, 0:SUBLANES, :] = jnp.zeros((2, SUBLANES, POOL_PAD), F32)

    @pl.when(s > 0)
    def _():
        hist_ref[0:POOL_HALO, :] = hist_ref[ts:ts + POOL_HALO, :]

    hist_ref[POOL_HALO:n_rows, :] = proj[:, :tok_w]

    def window_sum(cols, w):
        read = lambda lo, hi: hist_ref[lo:hi, cols]
        k = 1
        while 2 * k < w:
            slot = (k.bit_length() - 1) % 2
            lvl_ref[slot, SUBLANES:n_rows, :] = read(SUBLANES, n_rows) + read(SUBLANES - k, n_rows - k)
            read = lambda lo, hi, slot=slot: lvl_ref[slot, lo:hi, :]
            k *= 2
        return read(POOL_HALO, n_rows) + read(POOL_HALO - k, n_rows - k)

    pos = s * ts + lax.broadcasted_iota(jnp.int32, (ts, 1), 0)
    for i, w in enumerate(POOL_WINDOWS):
        cols = slice(i * POOL_PAD, (i + 1) * POOL_PAD)
        u = hist_ref[POOL_HALO:n_rows, cols]
        inv_count = 1.0 / jnp.minimum(pos + 1, w).astype(F32)
        pooled = window_sum(cols, w) * inv_count - u
        mixed = _dot(pooled.astype(BF16), pool_w_ref[i]) * pool_s_ref[i:i + 1, :]
        cat_ref[:, cols] = mixed.astype(BF16)

    mem_out = _memory_attention(proj[:, tok_w:].astype(BF16), kb_ref[0], vb_ref[0], ones_ref[...], m_len)
    cat_ref[:, tok_w:] = mem_out.astype(BF16)

    mix = _dot(cat_ref[...], w_out_ref[...])
    o_ref[0] = _layer_norm(DN_ALPHA * x + mix, g_ref[...], b_ref[...])


def _mix_a(x, w_in_pad, pool_w_pad, pool_s_pad, k_big, v_big, head_ones, w_out_pad, g, b, ts):
    B, S, D = x.shape
    wp = w_in_pad.shape[1]
    tok_w = N_POOL * POOL_PAD
    mm = k_big.shape[1]
    return pl.pallas_call(
        _mix_a_kernel,
        grid=(B, S // ts),
        in_specs=[pl.BlockSpec((1, ts, D), lambda bi, si: (bi, si, 0)),
                  _const_spec((D, wp)),
                  _const_spec((N_POOL, POOL_PAD, POOL_PAD)),
                  _const_spec((N_POOL, POOL_PAD)),
                  pl.BlockSpec((1, mm, MEM_WIDTH), lambda bi, si: (bi, 0, 0)),
                  pl.BlockSpec((1, mm, MEM_WIDTH), lambda bi, si: (bi, 0, 0)),
                  _const_spec(head_ones.shape),
                  _const_spec((wp, D)),
                  _const_spec((1, D)),
                  _const_spec((1, D))],
        out_specs=pl.BlockSpec((1, ts, D), lambda bi, si: (bi, si, 0)),
        out_shape=jax.ShapeDtypeStruct((B, S, D), F32),
        scratch_shapes=[pltpu.VMEM((POOL_HALO + ts, tok_w), F32),
                        pltpu.VMEM((2, POOL_HALO + ts, POOL_PAD), F32),
                        pltpu.VMEM((ts, wp), BF16)],
        compiler_params=_params(2),
        name="mix_a",
    )(x, w_in_pad, pool_w_pad, pool_s_pad, k_big, v_big, head_ones, w_out_pad, g, b)


def _ffn_kernel(fc, x_ref, wu_ref, wg_ref, cp_ref, wd_ref, g_ref, b_ref, o_ref,
                hu_ref, hg_ref, carry_ref, act_ref):
    ts = x_ref.shape[1]
    n_chunks = wu_ref.shape[1] // fc
    halo = SUBLANES
    s = pl.program_id(1)
    x = x_ref[0]
    xb = x.astype(BF16)

    @pl.when(s == 0)
    def _():
        carry_ref[...] = jnp.zeros(carry_ref.shape, F32)

    def conv(h_ref, slot, taps, bias):
        return (h_ref[slot, halo - 2:halo - 2 + ts, :] * taps[0:1]
                + h_ref[slot, halo - 1:halo - 1 + ts, :] * taps[1:2]
                + h_ref[slot, halo:halo + ts, :] * taps[2:3] + bias)

    def up_project(c):
        cols = slice(c * fc, (c + 1) * fc)
        slot = c % 2
        hu_ref[slot, 0:halo, :] = carry_ref[0, :, cols]
        hg_ref[slot, 0:halo, :] = carry_ref[1, :, cols]
        hu_ref[slot, halo:halo + ts, :] = _dot(xb, wu_ref[:, cols])
        hg_ref[slot, halo:halo + ts, :] = _dot(xb, wg_ref[:, cols])
        carry_ref[0, :, cols] = hu_ref[slot, ts:ts + halo, :]
        carry_ref[1, :, cols] = hg_ref[slot, ts:ts + halo, :]

    def gate_chunk(c):
        cols = slice(c * fc, (c + 1) * fc)
        slot = c % 2
        cp = cp_ref[:, cols]
        u = conv(hu_ref, slot, cp[0:3], cp[6:7])
        gate = conv(hg_ref, slot, cp[3:6], cp[7:8])
        half = 0.5 * gate
        act_ref[:, cols] = ((half + half * jnp.tanh(half)) * u).astype(BF16)

    group = -(-n_chunks // FFN_DOWN_GROUPS)
    ffn = None
    for c in range(n_chunks):
        up_project(c)
        gate_chunk(c)
        if (c + 1) % group == 0 or c + 1 == n_chunks:
            rows = slice((c // group) * group * fc, (c + 1) * fc)
            part = _dot(act_ref[:, rows], wd_ref[rows, :])
            ffn = part if ffn is None else ffn + part
    o_ref[0] = _layer_norm(DN_ALPHA * x + ffn, g_ref[...], b_ref[...])


def _ffn(x, wu, wg, cp, wd, g, b, ts, fc):
    B, S, D = x.shape
    fp = wu.shape[1]
    return pl.pallas_call(
        functools.partial(_ffn_kernel, fc),
        grid=(B, S // ts),
        in_specs=[pl.BlockSpec((1, ts, D), lambda bi, si: (bi, si, 0)),
                  _const_spec(wu.shape),
                  _const_spec(wg.shape),
                  _const_spec(cp.shape),
                  _const_spec(wd.shape),
                  _const_spec((1, D)),
                  _const_spec((1, D))],
        out_specs=pl.BlockSpec((1, ts, D), lambda bi, si: (bi, si, 0)),
        out_shape=jax.ShapeDtypeStruct((B, S, D), F32),
        scratch_shapes=[pltpu.VMEM((2, SUBLANES + ts, fc), F32),
                        pltpu.VMEM((2, SUBLANES + ts, fc), F32),
                        pltpu.VMEM((2, SUBLANES, fp), F32),
                        pltpu.VMEM((ts, fp), BF16)],
        compiler_params=_params(2),
        name="ffn",
    )(x, wu, wg, cp, wd, g, b)


def _qkv_kernel(x_ref, w_nat_ref, w_tr_ref, fb_ref, tri_ref, pqt_ref, pk_ref, cq_ref, ck_ref,
                qt_ref, qm_ref, k_ref, vt_ref, fqt_ref, fk_ref, carry_ref):
    ts = x_ref.shape[1]
    tok_w = k_ref.shape[2]
    s = pl.program_id(1)
    xb = x_ref[0].astype(BF16)
    nat = _dot(xb, w_nat_ref[...])
    qm_ref[0] = nat[:, :MEM_WIDTH].astype(BF16)
    k_ref[0] = nat[:, MEM_WIDTH:].astype(BF16)
    tr = _dot_nt(w_tr_ref[...], xb)
    qt_ref[0, 0] = tr[0:tok_w].astype(BF16)
    vt_ref[0, 0] = tr[tok_w:2 * tok_w].astype(BF16)

    @pl.when(s == 0)
    def _():
        carry_ref[...] = jnp.zeros(carry_ref.shape, F32)

    z = tr[2 * tok_w:] + fb_ref[...]
    log_f = jnp.minimum(z, 0.0) - jnp.log1p(jnp.exp(-jnp.abs(z)))
    part = _dot(jnp.concatenate(_split3(log_f), axis=0), tri_ref[...])
    cum_t = (part[0:LANES] + part[LANES:2 * LANES] + part[2 * LANES:]) + carry_ref[:, 0:1]
    carry_ref[:, 0:1] = cum_t[:, ts - 1:ts]
    dec_t = cum_t * LOG2E
    pieces_t = jnp.concatenate(_split3(dec_t), axis=0)
    fqt_ref[0, 0] = (_dot(pqt_ref[...], pieces_t) + cq_ref[...]).astype(BF16)
    pieces = jnp.concatenate(_split3(dec_t.T), axis=1)
    fk_ref[0] = (_dot(pieces, pk_ref[...]) + ck_ref[...]).astype(BF16)


def _qkv(x, w_nat, w_tr, fb_col, tri, pqt, pk, cq_col, ck, tok_w, ts):
    B, S, D = x.shape
    n_tiles = S // ts
    row = lambda width: pl.BlockSpec((1, ts, width), lambda bi, si: (bi, si, 0))
    row_shape = lambda width: jax.ShapeDtypeStruct((B, S, width), BF16)
    col = lambda height: pl.BlockSpec((1, 1, height, ts), lambda bi, si: (bi, si, 0, 0))
    col_shape = lambda height: jax.ShapeDtypeStruct((B, n_tiles, height, ts), BF16)
    return pl.pallas_call(
        _qkv_kernel,
        grid=(B, n_tiles),
        in_specs=[pl.BlockSpec((1, ts, D), lambda bi, si: (bi, si, 0)),
                  _const_spec(w_nat.shape), _const_spec(w_tr.shape), _const_spec(fb_col.shape),
                  _const_spec(tri.shape), _const_spec(pqt.shape), _const_spec(pk.shape),
                  _const_spec(cq_col.shape), _const_spec(ck.shape)],
        out_specs=[col(tok_w), row(MEM_WIDTH), row(tok_w), col(tok_w), col(LANES), row(LANES)],
        out_shape=[col_shape(tok_w), row_shape(MEM_WIDTH), row_shape(tok_w), col_shape(tok_w),
                   col_shape(LANES), row_shape(LANES)],
        scratch_shapes=[pltpu.VMEM((LANES, LANES), F32)],
        compiler_params=_params(2),
        name="qkv",
    )(x, w_nat, w_tr, fb_col, tri, pqt, pk, cq_col, ck)


def _fox_kernel(qt_ref, fqt_ref, k_ref, fk_ref, vt_ref, o_ref,
                kaug_ref, vaug_ref, qat_ref, sc_ref, m_ref, acc_ref):
    n_sub, tk = qt_ref.shape[1], qt_ref.shape[3]
    tq = n_sub * tk
    seq = k_ref.shape[1]
    j = pl.program_id(1)
    i = pl.program_id(2)

    @pl.when(i == 0)
    def _():
        lane = lax.broadcasted_iota(jnp.int32, (seq, LANES), 1)
        kp = k_ref[0]
        fk = fk_ref[0]
        zero = jnp.zeros((seq, LANES), BF16)
        vt = vt_ref[0]
        feat = lax.broadcasted_iota(jnp.int32, vt.shape, 1)
        one = jnp.ones(vt.shape, BF16)
        for h in range(2):
            head = 2 * j + h
            kaug_ref[h, :, 0:LANES] = jnp.where(lane // HEAD_DIM == h, kp, zero)
            in_head = (lane >= head * DECAY_LANES) & (lane < (head + 1) * DECAY_LANES)
            kaug_ref[h, :, LANES:] = jnp.where(in_head, fk, zero)
            vaug_ref[h] = jnp.where(feat // HEAD_DIM == h, vt, one)

    for c in range(n_sub):
        qat_ref[0:LANES, c * tk:(c + 1) * tk] = qt_ref[0, c]
        qat_ref[LANES:, c * tk:(c + 1) * tk] = fqt_ref[0, c]
    m_ref[...] = jnp.full(m_ref.shape, NEG_BIG, F32)
    acc_ref[...] = jnp.zeros(acc_ref.shape, F32)

    cw = sc_ref.shape[1]
    n_chunks = tq // cw

    def scores(h, cc, t):
        return _dot(kaug_ref[h, pl.ds(pl.multiple_of(t * tk, tk), tk), :],
                    qat_ref[:, cc * cw:(cc + 1) * cw])

    masks = {}

    def visible(diag_offset):
        if diag_offset not in masks:
            key = lax.broadcasted_iota(jnp.int32, (tk, cw), 0)
            qry = lax.broadcasted_iota(jnp.int32, (tk, cw), 1)
            masks[diag_offset] = qry + diag_offset >= key
        return masks[diag_offset]

    def fold(h, cc, t, sc, diag_offset):
        cols = slice(cc * cw, (cc + 1) * cw)
        if diag_offset is not None:
            sc = jnp.where(visible(diag_offset), sc, NEG_BIG)
        m_old = m_ref[h, :, cols]
        m_new = jnp.maximum(m_old, jnp.max(sc, axis=0, keepdims=True))
        a = jnp.exp2(m_old - m_new)
        p = jnp.exp2(sc - m_new).astype(BF16)
        acc_ref[h, :, cols] = a * acc_ref[h, :, cols] + _dot(vaug_ref[h, t], p)
        m_ref[h, :, cols] = m_new

    def run(steps, sc, after):
        for n, (h, cc, t, off) in enumerate(steps):
            nxt = steps[n + 1] if n + 1 < len(steps) else after
            sc_next = scores(*nxt[:3]) if nxt is not None else None
            fold(h, cc, t, sc, off)
            sc = sc_next
        return sc

    sc_ref[...] = scores(0, 0, 0)

    def below_diagonal(u, carry):
        tiles = [n_sub * u + c for c in range(n_sub)]
        steps = [(h, cc, t, None) for t in tiles for h in range(2) for cc in range(n_chunks)]
        sc_ref[...] = run(steps, sc_ref[...], (0, 0, n_sub * (u + 1)))
        return carry

    first = n_sub * i
    lax.fori_loop(0, i, below_diagonal, 0)

    steps = []
    for c in range(n_sub):
        for h in range(2):
            for cc in range(c * tk // cw, n_chunks):
                off = cc * cw - c * tk
                steps.append((h, cc, first + c, off if off < tk else None))
    run(steps, sc_ref[...], None)

    feat = lax.broadcasted_iota(jnp.int32, (LANES, tq), 0)
    acc0, acc1 = acc_ref[0], acc_ref[1]
    inv0 = 1.0 / acc0[HEAD_DIM:HEAD_DIM + 1]
    inv1 = 1.0 / acc1[0:1]
    out_t = jnp.where(feat < HEAD_DIM, acc0 * inv0, acc1 * inv1)
    o_ref[0] = out_t.T.astype(BF16)


def _fox(qt, fqt, k, fk, vt, tq):
    B, n_tiles, tok_w, tk = qt.shape
    S = n_tiles * tk
    n_pairs = tok_w // LANES
    n_sub = tq // tk
    return pl.pallas_call(
        _fox_kernel,
        grid=(B, n_pairs, S // tq),
        in_specs=[pl.BlockSpec((1, n_sub, LANES, tk), lambda b, j, i: (b, i, j, 0)),
                  pl.BlockSpec((1, n_sub, LANES, tk), lambda b, j, i: (b, i, 0, 0)),
                  pl.BlockSpec((1, S, LANES), lambda b, j, i: (b, 0, j)),
                  pl.BlockSpec((1, S, LANES), lambda b, j, i: (b, 0, 0)),
                  pl.BlockSpec((1, n_tiles, LANES, tk), lambda b, j, i: (b, 0, j, 0))],
        out_specs=pl.BlockSpec((1, tq, LANES), lambda b, j, i: (b, i, j)),
        out_shape=jax.ShapeDtypeStruct((B, S, tok_w), BF16),
        scratch_shapes=[pltpu.VMEM((2, S, 2 * LANES), BF16),
                        pltpu.VMEM((2, n_tiles, LANES, tk), BF16),
                        pltpu.VMEM((2 * LANES, tq), BF16),
                        pltpu.VMEM((tk, 2 * MXU_N), F32),
                        pltpu.VMEM((2, 1, tq), F32),
                        pltpu.VMEM((2, LANES, tq), F32)],
        compiler_params=_params(3),
        name="fox",
    )(qt, fqt, k, fk, vt)


def _mix_b_kernel(x_ref, tok_ref, qm_ref, kb_ref, vb_ref, ones_ref, w_tok_ref, w_mem_ref, g_ref, b_ref, o_ref):
    m_len = kb_ref.shape[1] // MEM_HEADS
    mem_out = _memory_attention(qm_ref[0], kb_ref[0], vb_ref[0], ones_ref[...], m_len)
    mix = _dot(tok_ref[0], w_tok_ref[...]) + _dot(mem_out.astype(BF16), w_mem_ref[...])
    o_ref[0] = _layer_norm(DN_ALPHA * x_ref[0] + mix, g_ref[...], b_ref[...])


def _mix_b(x, tok, qm, k_big, v_big, head_ones, w_tok, w_mem, g, b, ts):
    B, S, D = x.shape
    tok_w = tok.shape[2]
    mm = k_big.shape[1]
    return pl.pallas_call(
        _mix_b_kernel,
        grid=(B, S // ts),
        in_specs=[pl.BlockSpec((1, ts, D), lambda bi, si: (bi, si, 0)),
                  pl.BlockSpec((1, ts, tok_w), lambda bi, si: (bi, si, 0)),
                  pl.BlockSpec((1, ts, MEM_WIDTH), lambda bi, si: (bi, si, 0)),
                  pl.BlockSpec((1, mm, MEM_WIDTH), lambda bi, si: (bi, 0, 0)),
                  pl.BlockSpec((1, mm, MEM_WIDTH), lambda bi, si: (bi, 0, 0)),
                  _const_spec(head_ones.shape), _const_spec(w_tok.shape), _const_spec(w_mem.shape),
                  _const_spec((1, D)), _const_spec((1, D))],
        out_specs=pl.BlockSpec((1, ts, D), lambda bi, si: (bi, si, 0)),
        out_shape=jax.ShapeDtypeStruct((B, S, D), F32),
        compiler_params=_params(2),
        name="mix_b",
    )(x, tok, qm, k_big, v_big, head_ones, w_tok, w_mem, g, b)


def _prep_ffn(w_up, conv_w, conv_b, w_down, fc):
    f = w_up.shape[1] // 2
    fp = _round_up(f, fc)
    pad_cols = lambda a: jnp.pad(a, ((0, 0), (0, fp - f)))
    wu = pad_cols(w_up[:, :f]).astype(BF16)
    wg = pad_cols(w_up[:, f:]).astype(BF16)
    cp = pad_cols(jnp.concatenate([conv_w[:, :f], conv_w[:, f:], conv_b[None, :f], conv_b[None, f:]],
                                  axis=0))
    wd = jnp.pad(w_down, ((0, fp - f), (0, 0))).astype(BF16)
    return wu, wg, cp, wd


def _prep_mix_a(w_in, pool_w, pool_scale, w_out):
    d = w_in.shape[0]
    grp = pool_w.shape[1]
    tok_w = N_POOL * grp
    pad = POOL_PAD - grp
    w_tok = jnp.pad(w_in[:, :tok_w].reshape(d, N_POOL, grp), ((0, 0), (0, 0), (0, pad)))
    w_in_pad = jnp.concatenate([w_tok.reshape(d, N_POOL * POOL_PAD), w_in[:, tok_w:]], axis=1)
    pool_w_pad = jnp.pad(pool_w, ((0, 0), (0, pad), (0, pad)))
    pool_s_pad = jnp.pad(pool_scale.reshape(N_POOL, grp), ((0, 0), (0, pad)))
    o_tok = jnp.pad(w_out[:tok_w].reshape(N_POOL, grp, d), ((0, 0), (0, pad), (0, 0)))
    w_out_pad = jnp.concatenate([o_tok.reshape(N_POOL * POOL_PAD, d), w_out[tok_w:]], axis=0)
    return w_in_pad.astype(BF16), pool_w_pad.astype(BF16), pool_s_pad, w_out_pad.astype(BF16)


def _prep_qkv(w_q, kv_w, f_b, tok_w, ts):
    n_heads = tok_w // HEAD_DIM
    assert n_heads * DECAY_LANES <= LANES
    w_f = jnp.pad(kv_w[:, 2 * tok_w:], ((0, 0), (0, LANES - n_heads)))
    w_nat = jnp.concatenate([w_q[:, tok_w:], kv_w[:, :tok_w]], axis=1).astype(BF16)
    w_tr = jnp.concatenate([w_q[:, :tok_w] * (ATTN_SCALE * LOG2E), kv_w[:, tok_w:2 * tok_w], w_f],
                           axis=1).T.astype(BF16)
    fb_col = jnp.pad(f_b, (0, LANES - n_heads)).reshape(LANES, 1)
    tri = (jnp.arange(ts)[:, None] <= jnp.arange(ts)[None, :]).astype(BF16)
    src = jnp.arange(DECAY_PARTS * LANES)
    part, head = src // LANES, src % LANES
    dst = jnp.arange(LANES)
    valid = (head < n_heads)[:, None]
    pq = (valid & (dst[None, :] == (head * DECAY_LANES + part)[:, None])).astype(BF16)
    pk = -(valid & (dst[None, :] == (head * DECAY_LANES + DECAY_PARTS + part)[:, None])).astype(BF16)
    in_use = dst < n_heads * DECAY_LANES
    cq_col = (in_use & (dst % DECAY_LANES >= DECAY_PARTS)).astype(F32).reshape(LANES, 1)
    ck = (in_use & (dst % DECAY_LANES < DECAY_PARTS)).astype(F32).reshape(1, LANES)
    return w_nat, w_tr, fb_col, tri, pq.T, pk, cq_col, ck


ROW_TILE = 512
FFN_CHUNK = 256
FFN_ROW_TILE = 512
FFN_DOWN_GROUPS = 1
ATTN_TILE = 1024


def kernel(x, mem, a_w_in, a_pool_w, a_pool_scale, a_w_out, b_w_q, b_w_out, kv_w, f_b, mem_w_kv,
           ln1_g, ln1_b, ln2_g, ln2_b, ffn_w_up, ffn_conv_w, ffn_conv_b, ffn_w_down):
    B, S, D = x.shape
    n_a = a_w_in.shape[0]
    n_b = b_w_q.shape[0]
    tok_w = D - MEM_WIDTH
    ts = min(ROW_TILE, S)
    tq = min(ATTN_TILE, S)
    assert S % ts == 0 and S % tq == 0 and tq % ts == 0 and tok_w % LANES == 0

    k_big, v_big = _memkv(mem, mem_w_kv.astype(BF16))
    m_len = mem.shape[1]
    head_ones = (jnp.arange(MEM_HEADS * m_len)[:, None] // m_len
                 == jnp.arange(MEM_WIDTH)[None, :] // HEAD_DIM).astype(BF16)
    row = lambda a, l: a[l].reshape(1, D)

    shared = None
    for l in range(n_a + n_b):
        if l < n_a:
            w_in_pad, pool_w_pad, pool_s_pad, w_out_pad = _prep_mix_a(
                a_w_in[l], a_pool_w[l], a_pool_scale[l], a_w_out[l])
            x = _mix_a(x, w_in_pad, pool_w_pad, pool_s_pad, k_big[l], v_big[l], head_ones, w_out_pad,
                       row(ln1_g, l), row(ln1_b, l), ts)
        else:
            j = l - n_a
            qt, qm, k_new, vt_new, fqt_new, fk_new = _qkv(
                x, *_prep_qkv(b_w_q[j], kv_w, f_b, tok_w, ts), tok_w, ts)
            if j == 0:
                shared = (fqt_new, k_new, fk_new, vt_new)
            fqt, k_sh, fk_sh, vt_sh = shared
            tok = _fox(qt, fqt, k_sh, fk_sh, vt_sh, tq)
            w_out = b_w_out[j].astype(BF16)
            x = _mix_b(x, tok, qm, k_big[l], v_big[l], head_ones, w_out[:tok_w], w_out[tok_w:],
                       row(ln1_g, l), row(ln1_b, l), ts)
        wu, wg, cp, wd = _prep_ffn(ffn_w_up[l], ffn_conv_w[l], ffn_conv_b[l], ffn_w_down[l], FFN_CHUNK)
        x = _ffn(x, wu, wg, cp, wd, row(ln2_g, l), row(ln2_b, l), min(FFN_ROW_TILE, S), FFN_CHUNK)
    return x
```

```python
import functools

import jax
import jax.numpy as jnp
from jax import lax
from jax.experimental import pallas as pl
from jax.experimental.pallas import tpu as pltpu

F32 = jnp.float32
BF16 = jnp.bfloat16

HEAD_DIM = 64
MEM_HEADS = 4
MEM_WIDTH = MEM_HEADS * HEAD_DIM
POOL_WINDOWS = (2, 4, 8, 16)
N_POOL = len(POOL_WINDOWS)
CONV_WIDTH = 3
LN_EPS = 1e-5
DEPTH = 2
DN_ALPHA = (2.0 * DEPTH) ** 0.25
ATTN_SCALE = HEAD_DIM ** -0.5
LOG2E = 1.4426950408889634

LANES = 128
SUBLANES = 8
MXU_N = 256
VMEM_LIMIT_BYTES = 56 * 1024 * 1024

POOL_PAD = MXU_N
POOL_HALO = 24
DECAY_PARTS = 3
DECAY_LANES = 2 * DECAY_PARTS
NEG_BIG = -1e30


def _round_up(n, m):
    return (n + m - 1) // m * m


def _const_spec(shape):
    nd = len(shape)
    return pl.BlockSpec(shape, lambda *_: (0,) * nd, pipeline_mode=pl.Buffered(1))


def _params(n_grid):
    return pltpu.CompilerParams(dimension_semantics=("arbitrary",) * n_grid,
                                vmem_limit_bytes=VMEM_LIMIT_BYTES)


def _layer_norm(y, g, b):
    mu = jnp.mean(y, axis=-1, keepdims=True)
    d = y - mu
    var = jnp.mean(d * d, axis=-1, keepdims=True)
    return d * lax.rsqrt(var + LN_EPS) * g + b


def _dot(a, b):
    return jnp.dot(a, b, preferred_element_type=F32)


def _dot_nt(a, b):
    return lax.dot_general(a, b, (((1,), (1,)), ((), ())), preferred_element_type=F32)


def _split3(x):
    hi = x.astype(BF16)
    r = x - hi.astype(F32)
    mid = r.astype(BF16)
    lo = (r - mid.astype(F32)).astype(BF16)
    return hi, mid, lo


def _memkv_kernel(mem_ref, w_ref, k_ref, v_ref):
    m_len = mem_ref.shape[1]
    kv = _dot(mem_ref[0].astype(BF16), w_ref[0])
    k = kv[:, :MEM_WIDTH] * ATTN_SCALE
    v = kv[:, MEM_WIDTH:]
    head_of_lane = lax.broadcasted_iota(jnp.int32, (m_len, MEM_WIDTH), 1) // HEAD_DIM
    for h in range(MEM_HEADS):
        sel = head_of_lane == h
        k_ref[0, 0, h * m_len:(h + 1) * m_len, :] = jnp.where(sel, k, 0.0).astype(BF16)
        v_ref[0, 0, h * m_len:(h + 1) * m_len, :] = jnp.where(sel, v, 0.0).astype(BF16)


def _memkv(mem, w_kv_bf16):
    B, M, D = mem.shape
    L = w_kv_bf16.shape[0]
    out = jax.ShapeDtypeStruct((L, B, MEM_HEADS * M, MEM_WIDTH), BF16)
    out_spec = pl.BlockSpec((1, 1, MEM_HEADS * M, MEM_WIDTH), lambda l, b: (l, b, 0, 0))
    return pl.pallas_call(
        _memkv_kernel,
        grid=(L, B),
        in_specs=[pl.BlockSpec((1, M, D), lambda l, b: (b, 0, 0)),
                  pl.BlockSpec((1, D, 2 * MEM_WIDTH), lambda l, b: (l, 0, 0))],
        out_specs=[out_spec, out_spec],
        out_shape=[out, out],
        compiler_params=_params(2),
        name="memkv",
    )(mem, w_kv_bf16)


def _memory_attention(q_mem_bf16, k_big, v_big, head_ones, m_len):
    logits = _dot_nt(q_mem_bf16, k_big)
    weights = []
    for h in range(MEM_HEADS):
        lh = logits[:, h * m_len:(h + 1) * m_len]
        weights.append(jnp.exp(lh - jnp.max(lh, axis=-1, keepdims=True)).astype(BF16))
    e = jnp.concatenate(weights, axis=1)
    return _dot(e, v_big) / _dot(e, head_ones)


def _mix_a_kernel(x_ref, w_in_ref, pool_w_ref, pool_s_ref, kb_ref, vb_ref, ones_ref, w_out_ref, g_ref, b_ref,
                  o_ref, hist_ref, lvl_ref, cat_ref):
    ts = x_ref.shape[1]
    m_len = kb_ref.shape[1] // MEM_HEADS
    tok_w = N_POOL * POOL_PAD
    n_rows = POOL_HALO + ts
    s = pl.program_id(1)
    x = x_ref[0]
    proj = _dot(x.astype(BF16), w_in_ref[...])

    @pl.when(s == 0)
    def _():
        hist_ref[0:POOL_HALO, :] = jnp.zeros((POOL_HALO, tok_w), F32)
        lvl_ref[:, 0:SUBLANES, :] = jnp.zeros((2, SUBLANES, POOL_PAD), F32)

    @pl.when(s > 0)
    def _():
        hist_ref[0:POOL_HALO, :] = hist_ref[ts:ts + POOL_HALO, :]

    hist_ref[POOL_HALO:n_rows, :] = proj[:, :tok_w]

    def window_sum(cols, w):
        read = lambda lo, hi: hist_ref[lo:hi, cols]
        k = 1
        while 2 * k < w:
            slot = (k.bit_length() - 1) % 2
            lvl_ref[slot, SUBLANES:n_rows, :] = read(SUBLANES, n_rows) + read(SUBLANES - k, n_rows - k)
            read = lambda lo, hi, slot=slot: lvl_ref[slot, lo:hi, :]
            k *= 2
        return read(POOL_HALO, n_rows) + read(POOL_HALO - k, n_rows - k)

    pos = s * ts + lax.broadcasted_iota(jnp.int32, (ts, 1), 0)
    for i, w in enumerate(POOL_WINDOWS):
        cols = slice(i * POOL_PAD, (i + 1) * POOL_PAD)
        u = hist_ref[POOL_HALO:n_rows, cols]
        inv_count = 1.0 / jnp.minimum(pos + 1, w).astype(F32)
        pooled = window_sum(cols, w) * inv_count - u
        mixed = _dot(pooled.astype(BF16), pool_w_ref[i]) * pool_s_ref[i:i + 1, :]
        cat_ref[:, cols] = mixed.astype(BF16)

    mem_out = _memory_attention(proj[:, tok_w:].astype(BF16), kb_ref[0], vb_ref[0], ones_ref[...], m_len)
    cat_ref[:, tok_w:] = mem_out.astype(BF16)

    mix = _dot(cat_ref[...], w_out_ref[...])
    o_ref[0] = _layer_norm(DN_ALPHA * x + mix, g_ref[...], b_ref[...])


def _mix_a(x, w_in_pad, pool_w_pad, pool_s_pad, k_big, v_big, head_ones, w_out_pad, g, b, ts):
    B, S, D = x.shape
    wp = w_in_pad.shape[1]
    tok_w = N_POOL * POOL_PAD
    mm = k_big.shape[1]
    return pl.pallas_call(
        _mix_a_kernel,
        grid=(B, S // ts),
        in_specs=[pl.BlockSpec((1, ts, D), lambda bi, si: (bi, si, 0)),
                  _const_spec((D, wp)),
                  _const_spec((N_POOL, POOL_PAD, POOL_PAD)),
                  _const_spec((N_POOL, POOL_PAD)),
                  pl.BlockSpec((1, mm, MEM_WIDTH), lambda bi, si: (bi, 0, 0)),
                  pl.BlockSpec((1, mm, MEM_WIDTH), lambda bi, si: (bi, 0, 0)),
                  _const_spec(head_ones.shape),
                  _const_spec((wp, D)),
                  _const_spec((1, D)),
                  _const_spec((1, D))],
        out_specs=pl.BlockSpec((1, ts, D), lambda bi, si: (bi, si, 0)),
        out_shape=jax.ShapeDtypeStruct((B, S, D), F32),
        scratch_shapes=[pltpu.VMEM((POOL_HALO + ts, tok_w), F32),
                        pltpu.VMEM((2, POOL_HALO + ts, POOL_PAD), F32),
                        pltpu.VMEM((ts, wp), BF16)],
        compiler_params=_params(2),
        name="mix_a",
    )(x, w_in_pad, pool_w_pad, pool_s_pad, k_big, v_big, head_ones, w_out_pad, g, b)


def _ffn_kernel(fc, x_ref, wu_ref, wg_ref, cp_ref, wd_ref, g_ref, b_ref, o_ref,
                hu_ref, hg_ref, carry_ref, act_ref):
    ts = x_ref.shape[1]
    n_chunks = wu_ref.shape[1] // fc
    halo = SUBLANES
    s = pl.program_id(1)
    x = x_ref[0]
    xb = x.astype(BF16)

    @pl.when(s == 0)
    def _():
        carry_ref[...] = jnp.zeros(carry_ref.shape, F32)

    def conv(h_ref, slot, taps, bias):
        return (h_ref[slot, halo - 2:halo - 2 + ts, :] * taps[0:1]
                + h_ref[slot, halo - 1:halo - 1 + ts, :] * taps[1:2]
                + h_ref[slot, halo:halo + ts, :] * taps[2:3] + bias)

    def up_project(c):
        cols = slice(c * fc, (c + 1) * fc)
        slot = c % 2
        hu_ref[slot, 0:halo, :] = carry_ref[0, :, cols]
        hg_ref[slot, 0:halo, :] = carry_ref[1, :, cols]
        hu_ref[slot, halo:halo + ts, :] = _dot(xb, wu_ref[:, cols])
        hg_ref[slot, halo:halo + ts, :] = _dot(xb, wg_ref[:, cols])
        carry_ref[0, :, cols] = hu_ref[slot, ts:ts + halo, :]
        carry_ref[1, :, cols] = hg_ref[slot, ts:ts + halo, :]

    def gate_chunk(c):
        cols = slice(c * fc, (c + 1) * fc)
        slot = c % 2
        cp = cp_ref[:, cols]
        u = conv(hu_ref, slot, cp[0:3], cp[6:7])
        gate = conv(hg_ref, slot, cp[3:6], cp[7:8])
        half = 0.5 * gate
        act_ref[:, cols] = ((half + half * jnp.tanh(half)) * u).astype(BF16)

    group = -(-n_chunks // FFN_DOWN_GROUPS)
    ffn = None
    for c in range(n_chunks):
        up_project(c)
        gate_chunk(c)
        if (c + 1) % group == 0 or c + 1 == n_chunks:
            rows = slice((c // group) * group * fc, (c + 1) * fc)
            part = _dot(act_ref[:, rows], wd_ref[rows, :])
            ffn = part if ffn is None else ffn + part
    o_ref[0] = _layer_norm(DN_ALPHA * x + ffn, g_ref[...], b_ref[...])


def _ffn(x, wu, wg, cp, wd, g, b, ts, fc):
    B, S, D = x.shape
    fp = wu.shape[1]
    return pl.pallas_call(
        functools.partial(_ffn_kernel, fc),
        grid=(B, S // ts),
        in_specs=[pl.BlockSpec((1, ts, D), lambda bi, si: (bi, si, 0)),
                  _const_spec(wu.shape),
                  _const_spec(wg.shape),
                  _const_spec(cp.shape),
                  _const_spec(wd.shape),
                  _const_spec((1, D)),
                  _const_spec((1, D))],
        out_specs=pl.BlockSpec((1, ts, D), lambda bi, si: (bi, si, 0)),
        out_shape=jax.ShapeDtypeStruct((B, S, D), F32),
        scratch_shapes=[pltpu.VMEM((2, SUBLANES + ts, fc), F32),
                        pltpu.VMEM((2, SUBLANES + ts, fc), F32),
                        pltpu.VMEM((2, SUBLANES, fp), F32),
                        pltpu.VMEM((ts, fp), BF16)],
        compiler_params=_params(2),
        name="ffn",
    )(x, wu, wg, cp, wd, g, b)


def _qkv_kernel(x_ref, w_nat_ref, w_tr_ref, fb_ref, tri_ref, pqt_ref, pk_ref, cq_ref, ck_ref,
                qt_ref, qm_ref, k_ref, vt_ref, fqt_ref, fk_ref, carry_ref):
    ts = x_ref.shape[1]
    tok_w = k_ref.shape[2]
    s = pl.program_id(1)
    xb = x_ref[0].astype(BF16)
    nat = _dot(xb, w_nat_ref[...])
    qm_ref[0] = nat[:, :MEM_WIDTH].astype(BF16)
    k_ref[0] = nat[:, MEM_WIDTH:].astype(BF16)
    tr = _dot_nt(w_tr_ref[...], xb)
    qt_ref[0, 0] = tr[0:tok_w].astype(BF16)
    vt_ref[0, 0] = tr[tok_w:2 * tok_w].astype(BF16)

    @pl.when(s == 0)
    def _():
        carry_ref[...] = jnp.zeros(carry_ref.shape, F32)

    z = tr[2 * tok_w:] + fb_ref[...]
    log_f = jnp.minimum(z, 0.0) - jnp.log1p(jnp.exp(-jnp.abs(z)))
    part = _dot(jnp.concatenate(_split3(log_f), axis=0), tri_ref[...])
    cum_t = (part[0:LANES] + part[LANES:2 * LANES] + part[2 * LANES:]) + carry_ref[:, 0:1]
    carry_ref[:, 0:1] = cum_t[:, ts - 1:ts]
    dec_t = cum_t * LOG2E
    pieces_t = jnp.concatenate(_split3(dec_t), axis=0)
    fqt_ref[0, 0] = (_dot(pqt_ref[...], pieces_t) + cq_ref[...]).astype(BF16)
    pieces = jnp.concatenate(_split3(dec_t.T), axis=1)
    fk_ref[0] = (_dot(pieces, pk_ref[...]) + ck_ref[...]).astype(BF16)


def _qkv(x, w_nat, w_tr, fb_col, tri, pqt, pk, cq_col, ck, tok_w, ts):
    B, S, D = x.shape
    n_tiles = S // ts
    row = lambda width: pl.BlockSpec((1, ts, width), lambda bi, si: (bi, si, 0))
    row_shape = lambda width: jax.ShapeDtypeStruct((B, S, width), BF16)
    col = lambda height: pl.BlockSpec((1, 1, height, ts), lambda bi, si: (bi, si, 0, 0))
    col_shape = lambda height: jax.ShapeDtypeStruct((B, n_tiles, height, ts), BF16)
    return pl.pallas_call(
        _qkv_kernel,
        grid=(B, n_tiles),
        in_specs=[pl.BlockSpec((1, ts, D), lambda bi, si: (bi, si, 0)),
                  _const_spec(w_nat.shape), _const_spec(w_tr.shape), _const_spec(fb_col.shape),
                  _const_spec(tri.shape), _const_spec(pqt.shape), _const_spec(pk.shape),
                  _const_spec(cq_col.shape), _const_spec(ck.shape)],
        out_specs=[col(tok_w), row(MEM_WIDTH), row(tok_w), col(tok_w), col(LANES), row(LANES)],
        out_shape=[col_shape(tok_w), row_shape(MEM_WIDTH), row_shape(tok_w), col_shape(tok_w),
                   col_shape(LANES), row_shape(LANES)],
        scratch_shapes=[pltpu.VMEM((LANES, LANES), F32)],
        compiler_params=_params(2),
        name="qkv",
    )(x, w_nat, w_tr, fb_col, tri, pqt, pk, cq_col, ck)


def _fox_kernel(qt_ref, fqt_ref, k_ref, fk_ref, vt_ref, o_ref,
                kaug_ref, vaug_ref, qat_ref, sc_ref, m_ref, acc_ref):
    n_sub, tk = qt_ref.shape[1], qt_ref.shape[3]
    tq = n_sub * tk
    seq = k_ref.shape[1]
    j = pl.program_id(1)
    i = pl.program_id(2)

    @pl.when(i == 0)
    def _():
        lane = lax.broadcasted_iota(jnp.int32, (seq, LANES), 1)
        kp = k_ref[0]
        fk = fk_ref[0]
        zero = jnp.zeros((seq, LANES), BF16)
        vt = vt_ref[0]
        feat = lax.broadcasted_iota(jnp.int32, vt.shape, 1)
        one = jnp.ones(vt.shape, BF16)
        for h in range(2):
            head = 2 * j + h
            kaug_ref[h, :, 0:LANES] = jnp.where(lane // HEAD_DIM == h, kp, zero)
            in_head = (lane >= head * DECAY_LANES) & (lane < (head + 1) * DECAY_LANES)
            kaug_ref[h, :, LANES:] = jnp.where(in_head, fk, zero)
            vaug_ref[h] = jnp.where(feat // HEAD_DIM == h, vt, one)

    for c in range(n_sub):
        qat_ref[0:LANES, c * tk:(c + 1) * tk] = qt_ref[0, c]
        qat_ref[LANES:, c * tk:(c + 1) * tk] = fqt_ref[0, c]
    m_ref[...] = jnp.full(m_ref.shape, NEG_BIG, F32)
    acc_ref[...] = jnp.zeros(acc_ref.shape, F32)

    cw = sc_ref.shape[2]
    n_chunks = tq // cw

    def scores(h, cc, t):
        return _dot(kaug_ref[h, pl.ds(pl.multiple_of(t * tk, tk), tk), :],
                    qat_ref[:, cc * cw:(cc + 1) * cw])

    masks = {}

    def visible(diag_offset):
        if diag_offset not in masks:
            key = lax.broadcasted_iota(jnp.int32, (tk, cw), 0)
            qry = lax.broadcasted_iota(jnp.int32, (tk, cw), 1)
            masks[diag_offset] = qry + diag_offset >= key
        return masks[diag_offset]

    def fold(h, cc, t, sc, diag_offset):
        cols = slice(cc * cw, (cc + 1) * cw)
        if diag_offset is not None:
            sc = jnp.where(visible(diag_offset), sc, NEG_BIG)
        m_old = m_ref[h, :, cols]
        m_new = jnp.maximum(m_old, jnp.max(sc, axis=0, keepdims=True))
        a = jnp.exp2(m_old - m_new)
        p = jnp.exp2(sc - m_new).astype(BF16)
        acc_ref[h, :, cols] = a * acc_ref[h, :, cols] + _dot(vaug_ref[h, t], p)
        m_ref[h, :, cols] = m_new

    depth = sc_ref.shape[0]

    def run(steps, ready, after):
        queue = list(ready)
        todo = list(steps[depth:]) + list(after)
        for h, cc, t, off in steps:
            if todo:
                queue.append(scores(*todo.pop(0)[:3]))
            fold(h, cc, t, queue.pop(0), off)
        return queue

    def head_steps(t):
        return [(0, cc, t, None) for cc in range(depth)]

    for d, step in enumerate(head_steps(0)):
        sc_ref[d] = scores(*step[:3])

    def below_diagonal(u, carry):
        tiles = [n_sub * u + c for c in range(n_sub)]
        steps = [(h, cc, t, None) for t in tiles for h in range(2) for cc in range(n_chunks)]
        ahead = run(steps, [sc_ref[d] for d in range(depth)], head_steps(n_sub * (u + 1)))
        for d in range(depth):
            sc_ref[d] = ahead[d]
        return carry

    first = n_sub * i
    lax.fori_loop(0, i, below_diagonal, 0)

    steps = []
    for c in range(n_sub):
        for h in range(2):
            for cc in range(c * tk // cw, n_chunks):
                off = cc * cw - c * tk
                steps.append((h, cc, first + c, off if off < tk else None))
    assert [s[:2] for s in steps[:depth]] == [s[:2] for s in head_steps(first)]
    run(steps, [sc_ref[d] for d in range(depth)], [])

    feat = lax.broadcasted_iota(jnp.int32, (LANES, tq), 0)
    acc0, acc1 = acc_ref[0], acc_ref[1]
    inv0 = 1.0 / acc0[HEAD_DIM:HEAD_DIM + 1]
    inv1 = 1.0 / acc1[0:1]
    out_t = jnp.where(feat < HEAD_DIM, acc0 * inv0, acc1 * inv1)
    o_ref[0] = out_t.T.astype(BF16)


def _fox(qt, fqt, k, fk, vt, tq):
    B, n_tiles, tok_w, tk = qt.shape
    S = n_tiles * tk
    n_pairs = tok_w // LANES
    n_sub = tq // tk
    return pl.pallas_call(
        _fox_kernel,
        grid=(B, n_pairs, S // tq),
        in_specs=[pl.BlockSpec((1, n_sub, LANES, tk), lambda b, j, i: (b, i, j, 0)),
                  pl.BlockSpec((1, n_sub, LANES, tk), lambda b, j, i: (b, i, 0, 0)),
                  pl.BlockSpec((1, S, LANES), lambda b, j, i: (b, 0, j)),
                  pl.BlockSpec((1, S, LANES), lambda b, j, i: (b, 0, 0)),
                  pl.BlockSpec((1, n_tiles, LANES, tk), lambda b, j, i: (b, 0, j, 0))],
        out_specs=pl.BlockSpec((1, tq, LANES), lambda b, j, i: (b, i, j)),
        out_shape=jax.ShapeDtypeStruct((B, S, tok_w), BF16),
        scratch_shapes=[pltpu.VMEM((2, S, 2 * LANES), BF16),
                        pltpu.VMEM((2, n_tiles, LANES, tk), BF16),
                        pltpu.VMEM((2 * LANES, tq), BF16),
                        pltpu.VMEM((2, tk, 2 * MXU_N), F32),
                        pltpu.VMEM((2, 1, tq), F32),
                        pltpu.VMEM((2, LANES, tq), F32)],
        compiler_params=_params(3),
        name="fox",
    )(qt, fqt, k, fk, vt)


def _mix_b_kernel(x_ref, tok_ref, qm_ref, kb_ref, vb_ref, ones_ref, w_tok_ref, w_mem_ref, g_ref, b_ref, o_ref):
    m_len = kb_ref.shape[1] // MEM_HEADS
    mem_out = _memory_attention(qm_ref[0], kb_ref[0], vb_ref[0], ones_ref[...], m_len)
    mix = _dot(tok_ref[0], w_tok_ref[...]) + _dot(mem_out.astype(BF16), w_mem_ref[...])
    o_ref[0] = _layer_norm(DN_ALPHA * x_ref[0] + mix, g_ref[...], b_ref[...])


def _mix_b(x, tok, qm, k_big, v_big, head_ones, w_tok, w_mem, g, b, ts):
    B, S, D = x.shape
    tok_w = tok.shape[2]
    mm = k_big.shape[1]
    return pl.pallas_call(
        _mix_b_kernel,
        grid=(B, S // ts),
        in_specs=[pl.BlockSpec((1, ts, D), lambda bi, si: (bi, si, 0)),
                  pl.BlockSpec((1, ts, tok_w), lambda bi, si: (bi, si, 0)),
                  pl.BlockSpec((1, ts, MEM_WIDTH), lambda bi, si: (bi, si, 0)),
                  pl.BlockSpec((1, mm, MEM_WIDTH), lambda bi, si: (bi, 0, 0)),
                  pl.BlockSpec((1, mm, MEM_WIDTH), lambda bi, si: (bi, 0, 0)),
                  _const_spec(head_ones.shape), _const_spec(w_tok.shape), _const_spec(w_mem.shape),
                  _const_spec((1, D)), _const_spec((1, D))],
        out_specs=pl.BlockSpec((1, ts, D), lambda bi, si: (bi, si, 0)),
        out_shape=jax.ShapeDtypeStruct((B, S, D), F32),
        compiler_params=_params(2),
        name="mix_b",
    )(x, tok, qm, k_big, v_big, head_ones, w_tok, w_mem, g, b)


def _prep_ffn(w_up, conv_w, conv_b, w_down, fc):
    f = w_up.shape[1] // 2
    fp = _round_up(f, fc)
    pad_cols = lambda a: jnp.pad(a, ((0, 0), (0, fp - f)))
    wu = pad_cols(w_up[:, :f]).astype(BF16)
    wg = pad_cols(w_up[:, f:]).astype(BF16)
    cp = pad_cols(jnp.concatenate([conv_w[:, :f], conv_w[:, f:], conv_b[None, :f], conv_b[None, f:]],
                                  axis=0))
    wd = jnp.pad(w_down, ((0, fp - f), (0, 0))).astype(BF16)
    return wu, wg, cp, wd


def _prep_mix_a(w_in, pool_w, pool_scale, w_out):
    d = w_in.shape[0]
    grp = pool_w.shape[1]
    tok_w = N_POOL * grp
    pad = POOL_PAD - grp
    w_tok = jnp.pad(w_in[:, :tok_w].reshape(d, N_POOL, grp), ((0, 0), (0, 0), (0, pad)))
    w_in_pad = jnp.concatenate([w_tok.reshape(d, N_POOL * POOL_PAD), w_in[:, tok_w:]], axis=1)
    pool_w_pad = jnp.pad(pool_w, ((0, 0), (0, pad), (0, pad)))
    pool_s_pad = jnp.pad(pool_scale.reshape(N_POOL, grp), ((0, 0), (0, pad)))
    o_tok = jnp.pad(w_out[:tok_w].reshape(N_POOL, grp, d), ((0, 0), (0, pad), (0, 0)))
    w_out_pad = jnp.concatenate([o_tok.reshape(N_POOL * POOL_PAD, d), w_out[tok_w:]], axis=0)
    return w_in_pad.astype(BF16), pool_w_pad.astype(BF16), pool_s_pad, w_out_pad.astype(BF16)


def _prep_qkv(w_q, kv_w, f_b, tok_w, ts):
    n_heads = tok_w // HEAD_DIM
    assert n_heads * DECAY_LANES <= LANES
    w_f = jnp.pad(kv_w[:, 2 * tok_w:], ((0, 0), (0, LANES - n_heads)))
    w_nat = jnp.concatenate([w_q[:, tok_w:], kv_w[:, :tok_w]], axis=1).astype(BF16)
    w_tr = jnp.concatenate([w_q[:, :tok_w] * (ATTN_SCALE * LOG2E), kv_w[:, tok_w:2 * tok_w], w_f],
                           axis=1).T.astype(BF16)
    fb_col = jnp.pad(f_b, (0, LANES - n_heads)).reshape(LANES, 1)
    tri = (jnp.arange(ts)[:, None] <= jnp.arange(ts)[None, :]).astype(BF16)
    src = jnp.arange(DECAY_PARTS * LANES)
    part, head = src // LANES, src % LANES
    dst = jnp.arange(LANES)
    valid = (head < n_heads)[:, None]
    pq = (valid & (dst[None, :] == (head * DECAY_LANES + part)[:, None])).astype(BF16)
    pk = -(valid & (dst[None, :] == (head * DECAY_LANES + DECAY_PARTS + part)[:, None])).astype(BF16)
    in_use = dst < n_heads * DECAY_LANES
    cq_col = (in_use & (dst % DECAY_LANES >= DECAY_PARTS)).astype(F32).reshape(LANES, 1)
    ck = (in_use & (dst % DECAY_LANES < DECAY_PARTS)).astype(F32).reshape(1, LANES)
    return w_nat, w_tr, fb_col, tri, pq.T, pk, cq_col, ck


ROW_TILE = 512
FFN_CHUNK = 256
FFN_ROW_TILE = 512
MIX_ROW_TILE = 1024
FFN_DOWN_GROUPS = 1
ATTN_TILE = 2048


def kernel(x, mem, a_w_in, a_pool_w, a_pool_scale, a_w_out, b_w_q, b_w_out, kv_w, f_b, mem_w_kv,
           ln1_g, ln1_b, ln2_g, ln2_b, ffn_w_up, ffn_conv_w, ffn_conv_b, ffn_w_down):
    B, S, D = x.shape
    n_a = a_w_in.shape[0]
    n_b = b_w_q.shape[0]
    tok_w = D - MEM_WIDTH
    ts = min(ROW_TILE, S)
    tq = min(ATTN_TILE, S)
    assert S % ts == 0 and S % tq == 0 and tq % ts == 0 and tok_w % LANES == 0

    k_big, v_big = _memkv(mem, mem_w_kv.astype(BF16))
    m_len = mem.shape[1]
    head_ones = (jnp.arange(MEM_HEADS * m_len)[:, None] // m_len
                 == jnp.arange(MEM_WIDTH)[None, :] // HEAD_DIM).astype(BF16)
    row = lambda a, l: a[l].reshape(1, D)

    shared = None
    for l in range(n_a + n_b):
        if l < n_a:
            w_in_pad, pool_w_pad, pool_s_pad, w_out_pad = _prep_mix_a(
                a_w_in[l], a_pool_w[l], a_pool_scale[l], a_w_out[l])
            x = _mix_a(x, w_in_pad, pool_w_pad, pool_s_pad, k_big[l], v_big[l], head_ones, w_out_pad,
                       row(ln1_g, l), row(ln1_b, l), min(MIX_ROW_TILE, S))
        else:
            j = l - n_a
            qt, qm, k_new, vt_new, fqt_new, fk_new = _qkv(
                x, *_prep_qkv(b_w_q[j], kv_w, f_b, tok_w, ts), tok_w, ts)
            if j == 0:
                shared = (fqt_new, k_new, fk_new, vt_new)
            fqt, k_sh, fk_sh, vt_sh = shared
            tok = _fox(qt, fqt, k_sh, fk_sh, vt_sh, tq)
            w_out = b_w_out[j].astype(BF16)
            x = _mix_b(x, tok, qm, k_big[l], v_big[l], head_ones, w_out[:tok_w], w_out[tok_w:],
                       row(ln1_g, l), row(ln1_b, l), min(MIX_ROW_TILE, S))
        wu, wg, cp, wd = _prep_ffn(ffn_w_up[l], ffn_conv_w[l], ffn_conv_b[l], ffn_w_down[l], FFN_CHUNK)
        x = _ffn(x, wu, wg, cp, wd, row(ln2_g, l), row(ln2_b, l), min(FFN_ROW_TILE, S), FFN_CHUNK)
    return x
```

```python
import functools

import jax
import jax.numpy as jnp
from jax import lax
from jax.experimental import pallas as pl
from jax.experimental.pallas import tpu as pltpu

F32 = jnp.float32
BF16 = jnp.bfloat16

HEAD_DIM = 64
MEM_HEADS = 4
MEM_WIDTH = MEM_HEADS * HEAD_DIM
POOL_WINDOWS = (2, 4, 8, 16)
N_POOL = len(POOL_WINDOWS)
CONV_WIDTH = 3
LN_EPS = 1e-5
DEPTH = 2
DN_ALPHA = (2.0 * DEPTH) ** 0.25
ATTN_SCALE = HEAD_DIM ** -0.5
LOG2E = 1.4426950408889634

LANES = 128
SUBLANES = 8
MXU_N = 256
VMEM_LIMIT_BYTES = 56 * 1024 * 1024

POOL_PAD = MXU_N
POOL_HALO = 24
DECAY_PARTS = 3
DECAY_LANES = 2 * DECAY_PARTS
DEN_ROWS = 16
NEG_BIG = -1e30


def _round_up(n, m):
    return (n + m - 1) // m * m


def _const_spec(shape):
    nd = len(shape)
    return pl.BlockSpec(shape, lambda *_: (0,) * nd, pipeline_mode=pl.Buffered(1))


def _params(n_grid):
    return pltpu.CompilerParams(dimension_semantics=("arbitrary",) * n_grid,
                                vmem_limit_bytes=VMEM_LIMIT_BYTES)


def _layer_norm(y, g, b):
    mu = jnp.mean(y, axis=-1, keepdims=True)
    d = y - mu
    var = jnp.mean(d * d, axis=-1, keepdims=True)
    return d * lax.rsqrt(var + LN_EPS) * g + b


def _dot(a, b):
    return jnp.dot(a, b, preferred_element_type=F32)


def _dot_nt(a, b):
    return lax.dot_general(a, b, (((1,), (1,)), ((), ())), preferred_element_type=F32)


def _split3(x):
    hi = x.astype(BF16)
    r = x - hi.astype(F32)
    mid = r.astype(BF16)
    lo = (r - mid.astype(F32)).astype(BF16)
    return hi, mid, lo


def _memkv_kernel(mem_ref, w_ref, k_ref, v_ref):
    m_len = mem_ref.shape[1]
    kv = _dot(mem_ref[0].astype(BF16), w_ref[0])
    k = kv[:, :MEM_WIDTH] * ATTN_SCALE
    v = kv[:, MEM_WIDTH:]
    head_of_lane = lax.broadcasted_iota(jnp.int32, (m_len, MEM_WIDTH), 1) // HEAD_DIM
    for h in range(MEM_HEADS):
        sel = head_of_lane == h
        k_ref[0, 0, h * m_len:(h + 1) * m_len, :] = jnp.where(sel, k, 0.0).astype(BF16)
        v_ref[0, 0, h * m_len:(h + 1) * m_len, :] = jnp.where(sel, v, 0.0).astype(BF16)


def _memkv(mem, w_kv_bf16):
    B, M, D = mem.shape
    L = w_kv_bf16.shape[0]
    out = jax.ShapeDtypeStruct((L, B, MEM_HEADS * M, MEM_WIDTH), BF16)
    out_spec = pl.BlockSpec((1, 1, MEM_HEADS * M, MEM_WIDTH), lambda l, b: (l, b, 0, 0))
    return pl.pallas_call(
        _memkv_kernel,
        grid=(L, B),
        in_specs=[pl.BlockSpec((1, M, D), lambda l, b: (b, 0, 0)),
                  pl.BlockSpec((1, D, 2 * MEM_WIDTH), lambda l, b: (l, 0, 0))],
        out_specs=[out_spec, out_spec],
        out_shape=[out, out],
        compiler_params=_params(2),
        name="memkv",
    )(mem, w_kv_bf16)


def _memory_attention(q_mem_bf16, k_big, v_big, head_ones, m_len):
    logits = _dot_nt(q_mem_bf16, k_big)
    weights = []
    for h in range(MEM_HEADS):
        lh = logits[:, h * m_len:(h + 1) * m_len]
        weights.append(jnp.exp(lh - jnp.max(lh, axis=-1, keepdims=True)).astype(BF16))
    e = jnp.concatenate(weights, axis=1)
    return _dot(e, v_big) / _dot(e, head_ones)


def _mix_a_kernel(x_ref, w_in_ref, pool_w_ref, pool_s_ref, kb_ref, vb_ref, ones_ref, w_out_ref, g_ref, b_ref,
                  o_ref, hist_ref, lvl_ref, cat_ref):
    ts = x_ref.shape[1]
    m_len = kb_ref.shape[1] // MEM_HEADS
    tok_w = N_POOL * POOL_PAD
    n_rows = POOL_HALO + ts
    s = pl.program_id(1)
    x = x_ref[0]
    proj = _dot(x.astype(BF16), w_in_ref[...])

    @pl.when(s == 0)
    def _():
        hist_ref[0:POOL_HALO, :] = jnp.zeros((POOL_HALO, tok_w), F32)
        lvl_ref[:, 0:SUBLANES, :] = jnp.zeros((2, SUBLANES, POOL_PAD), F32)

    @pl.when(s > 0)
    def _():
        hist_ref[0:POOL_HALO, :] = hist_ref[ts:ts + POOL_HALO, :]

    hist_ref[POOL_HALO:n_rows, :] = proj[:, :tok_w]

    def window_sum(cols, w):
        read = lambda lo, hi: hist_ref[lo:hi, cols]
        k = 1
        while 2 * k < w:
            slot = (k.bit_length() - 1) % 2
            lvl_ref[slot, SUBLANES:n_rows, :] = read(SUBLANES, n_rows) + read(SUBLANES - k, n_rows - k)
            read = lambda lo, hi, slot=slot: lvl_ref[slot, lo:hi, :]
            k *= 2
        return read(POOL_HALO, n_rows) + read(POOL_HALO - k, n_rows - k)

    pos = s * ts + lax.broadcasted_iota(jnp.int32, (ts, 1), 0)
    for i, w in enumerate(POOL_WINDOWS):
        cols = slice(i * POOL_PAD, (i + 1) * POOL_PAD)
        u = hist_ref[POOL_HALO:n_rows, cols]
        inv_count = 1.0 / jnp.minimum(pos + 1, w).astype(F32)
        pooled = window_sum(cols, w) * inv_count - u
        mixed = _dot(pooled.astype(BF16), pool_w_ref[i]) * pool_s_ref[i:i + 1, :]
        cat_ref[:, cols] = mixed.astype(BF16)

    mem_out = _memory_attention(proj[:, tok_w:].astype(BF16), kb_ref[0], vb_ref[0], ones_ref[...], m_len)
    cat_ref[:, tok_w:] = mem_out.astype(BF16)

    mix = _dot(cat_ref[...], w_out_ref[...])
    o_ref[0] = _layer_norm(DN_ALPHA * x + mix, g_ref[...], b_ref[...])


def _mix_a(x, w_in_pad, pool_w_pad, pool_s_pad, k_big, v_big, head_ones, w_out_pad, g, b, ts):
    B, S, D = x.shape
    wp = w_in_pad.shape[1]
    tok_w = N_POOL * POOL_PAD
    mm = k_big.shape[1]
    return pl.pallas_call(
        _mix_a_kernel,
        grid=(B, S // ts),
        in_specs=[pl.BlockSpec((1, ts, D), lambda bi, si: (bi, si, 0)),
                  _const_spec((D, wp)),
                  _const_spec((N_POOL, POOL_PAD, POOL_PAD)),
                  _const_spec((N_POOL, POOL_PAD)),
                  pl.BlockSpec((1, mm, MEM_WIDTH), lambda bi, si: (bi, 0, 0)),
                  pl.BlockSpec((1, mm, MEM_WIDTH), lambda bi, si: (bi, 0, 0)),
                  _const_spec(head_ones.shape),
                  _const_spec((wp, D)),
                  _const_spec((1, D)),
                  _const_spec((1, D))],
        out_specs=pl.BlockSpec((1, ts, D), lambda bi, si: (bi, si, 0)),
        out_shape=jax.ShapeDtypeStruct((B, S, D), F32),
        scratch_shapes=[pltpu.VMEM((POOL_HALO + ts, tok_w), F32),
                        pltpu.VMEM((2, POOL_HALO + ts, POOL_PAD), F32),
                        pltpu.VMEM((ts, wp), BF16)],
        compiler_params=_params(2),
        name="mix_a",
    )(x, w_in_pad, pool_w_pad, pool_s_pad, k_big, v_big, head_ones, w_out_pad, g, b)


def _ffn_kernel(fc, n_sub, x_ref, wu_ref, wg_ref, cp_ref, wd_ref, g_ref, b_ref, o_ref,
                hu_ref, hg_ref, carry_ref, act_ref):
    rows = x_ref.shape[1] // n_sub
    n_chunks = wu_ref.shape[1] // fc
    halo = SUBLANES
    s = pl.program_id(1)

    @pl.when(s == 0)
    def _():
        carry_ref[...] = jnp.zeros(carry_ref.shape, F32)

    def conv(h_ref, slot, taps, bias):
        return (h_ref[slot, halo - 2:halo - 2 + rows, :] * taps[0:1]
                + h_ref[slot, halo - 1:halo - 1 + rows, :] * taps[1:2]
                + h_ref[slot, halo:halo + rows, :] * taps[2:3] + bias)

    for k in range(n_sub):
        x = x_ref[0, k * rows:(k + 1) * rows, :]
        xb = x.astype(BF16)
        for c in range(n_chunks):
            cols = slice(c * fc, (c + 1) * fc)
            slot = (k * n_chunks + c) % 2
            hu_ref[slot, 0:halo, :] = carry_ref[0, :, cols]
            hg_ref[slot, 0:halo, :] = carry_ref[1, :, cols]
            hu_ref[slot, halo:halo + rows, :] = _dot(xb, wu_ref[:, cols])
            hg_ref[slot, halo:halo + rows, :] = _dot(xb, wg_ref[:, cols])
            carry_ref[0, :, cols] = hu_ref[slot, rows:rows + halo, :]
            carry_ref[1, :, cols] = hg_ref[slot, rows:rows + halo, :]
            cp = cp_ref[:, cols]
            u = conv(hu_ref, slot, cp[0:3], cp[6:7])
            gate = conv(hg_ref, slot, cp[3:6], cp[7:8])
            half = 0.5 * gate
            act_ref[k, :, cols] = ((half + half * jnp.tanh(half)) * u).astype(BF16)
        ffn = _dot(act_ref[k], wd_ref[...])
        o_ref[0, k * rows:(k + 1) * rows, :] = _layer_norm(DN_ALPHA * x + ffn, g_ref[...], b_ref[...])


def _ffn(x, wu, wg, cp, wd, g, b, ts, fc, n_sub):
    B, S, D = x.shape
    fp = wu.shape[1]
    rows = ts // n_sub
    assert rows * n_sub == ts and rows % SUBLANES == 0
    return pl.pallas_call(
        functools.partial(_ffn_kernel, fc, n_sub),
        grid=(B, S // ts),
        in_specs=[pl.BlockSpec((1, ts, D), lambda bi, si: (bi, si, 0)),
                  _const_spec(wu.shape),
                  _const_spec(wg.shape),
                  _const_spec(cp.shape),
                  _const_spec(wd.shape),
                  _const_spec((1, D)),
                  _const_spec((1, D))],
        out_specs=pl.BlockSpec((1, ts, D), lambda bi, si: (bi, si, 0)),
        out_shape=jax.ShapeDtypeStruct((B, S, D), F32),
        scratch_shapes=[pltpu.VMEM((2, SUBLANES + rows, fc), F32),
                        pltpu.VMEM((2, SUBLANES + rows, fc), F32),
                        pltpu.VMEM((2, SUBLANES, fp), F32),
                        pltpu.VMEM((n_sub, rows, fp), BF16)],
        compiler_params=_params(2),
        name="ffn",
    )(x, wu, wg, cp, wd, g, b)


def _qkv_kernel(x_ref, w_nat_ref, w_tr_ref, fb_ref, tri_ref, pqt_ref, pk_ref, cq_ref, ck_ref,
                qt_ref, qm_ref, k_ref, vt_ref, fqt_ref, fk_ref, carry_ref):
    ts = x_ref.shape[1]
    tok_w = k_ref.shape[2]
    s = pl.program_id(1)
    xb = x_ref[0].astype(BF16)
    nat = _dot(xb, w_nat_ref[...])
    qm_ref[0] = nat[:, :MEM_WIDTH].astype(BF16)
    k_ref[0] = nat[:, MEM_WIDTH:].astype(BF16)
    tr = _dot_nt(w_tr_ref[...], xb)
    qt_ref[0, 0] = tr[0:tok_w].astype(BF16)
    vt_ref[0, 0] = tr[tok_w:2 * tok_w].astype(BF16)

    @pl.when(s == 0)
    def _():
        carry_ref[...] = jnp.zeros(carry_ref.shape, F32)

    z = tr[2 * tok_w:] + fb_ref[...]
    log_f = jnp.minimum(z, 0.0) - jnp.log1p(jnp.exp(-jnp.abs(z)))
    part = _dot(jnp.concatenate(_split3(log_f), axis=0), tri_ref[...])
    cum_t = (part[0:LANES] + part[LANES:2 * LANES] + part[2 * LANES:]) + carry_ref[:, 0:1]
    carry_ref[:, 0:1] = cum_t[:, ts - 1:ts]
    dec_t = cum_t * LOG2E
    pieces_t = jnp.concatenate(_split3(dec_t), axis=0)
    fqt_ref[0, 0] = (_dot(pqt_ref[...], pieces_t) + cq_ref[...]).astype(BF16)
    pieces = jnp.concatenate(_split3(dec_t.T), axis=1)
    fk_ref[0] = (_dot(pieces, pk_ref[...]) + ck_ref[...]).astype(BF16)


def _qkv(x, w_nat, w_tr, fb_col, tri, pqt, pk, cq_col, ck, tok_w, ts):
    B, S, D = x.shape
    n_tiles = S // ts
    row = lambda width: pl.BlockSpec((1, ts, width), lambda bi, si: (bi, si, 0))
    row_shape = lambda width: jax.ShapeDtypeStruct((B, S, width), BF16)
    col = lambda height: pl.BlockSpec((1, 1, height, ts), lambda bi, si: (bi, si, 0, 0))
    col_shape = lambda height: jax.ShapeDtypeStruct((B, n_tiles, height, ts), BF16)
    return pl.pallas_call(
        _qkv_kernel,
        grid=(B, n_tiles),
        in_specs=[pl.BlockSpec((1, ts, D), lambda bi, si: (bi, si, 0)),
                  _const_spec(w_nat.shape), _const_spec(w_tr.shape), _const_spec(fb_col.shape),
                  _const_spec(tri.shape), _const_spec(pqt.shape), _const_spec(pk.shape),
                  _const_spec(cq_col.shape), _const_spec(ck.shape)],
        out_specs=[col(tok_w), row(MEM_WIDTH), row(tok_w), col(tok_w), col(LANES), row(LANES)],
        out_shape=[col_shape(tok_w), row_shape(MEM_WIDTH), row_shape(tok_w), col_shape(tok_w),
                   col_shape(LANES), row_shape(LANES)],
        scratch_shapes=[pltpu.VMEM((LANES, LANES), F32)],
        compiler_params=_params(2),
        name="qkv",
    )(x, w_nat, w_tr, fb_col, tri, pqt, pk, cq_col, ck)


def _fox_kernel(qt_ref, fqt_ref, k_ref, fk_ref, vt_ref, o_ref,
                kaug_ref, vaug_ref, qat_ref, sc_ref, m_ref, acc_ref):
    n_sub, tk = qt_ref.shape[1], qt_ref.shape[3]
    tq = n_sub * tk
    seq = k_ref.shape[1]
    j = pl.program_id(1)
    i = pl.program_id(2)

    @pl.when(i == 0)
    def _():
        lane = lax.broadcasted_iota(jnp.int32, (seq, LANES), 1)
        kp = k_ref[0]
        fk = fk_ref[0]
        zero = jnp.zeros((seq, LANES), BF16)
        vt = vt_ref[0]
        feat = lax.broadcasted_iota(jnp.int32, vt.shape, 1)
        one = jnp.ones(vt.shape, BF16)
        for h in range(2):
            head = 2 * j + h
            kaug_ref[h, :, 0:LANES] = jnp.where(lane // HEAD_DIM == h, kp, zero)
            in_head = (lane >= head * DECAY_LANES) & (lane < (head + 1) * DECAY_LANES)
            kaug_ref[h, :, LANES:] = jnp.where(in_head, fk, zero)
            own = jnp.where(feat // HEAD_DIM == h, vt, one)
            lo = h * (HEAD_DIM - DEN_ROWS)
            vaug_ref[h] = own[:, lo:lo + HEAD_DIM + DEN_ROWS, :]

    for c in range(n_sub):
        qat_ref[0:LANES, c * tk:(c + 1) * tk] = qt_ref[0, c]
        qat_ref[LANES:, c * tk:(c + 1) * tk] = fqt_ref[0, c]
    m_ref[...] = jnp.full(m_ref.shape, NEG_BIG, F32)
    acc_ref[...] = jnp.zeros(acc_ref.shape, F32)

    cw = sc_ref.shape[2]
    n_chunks = tq // cw

    def scores(h, cc, t):
        return _dot(kaug_ref[h, pl.ds(pl.multiple_of(t * tk, tk), tk), :],
                    qat_ref[:, cc * cw:(cc + 1) * cw])

    masks = {}

    def visible(diag_offset):
        if diag_offset not in masks:
            key = lax.broadcasted_iota(jnp.int32, (tk, cw), 0)
            qry = lax.broadcasted_iota(jnp.int32, (tk, cw), 1)
            masks[diag_offset] = qry + diag_offset >= key
        return masks[diag_offset]

    def fold(h, cc, t, sc, diag_offset):
        cols = slice(cc * cw, (cc + 1) * cw)
        if diag_offset is not None:
            sc = jnp.where(visible(diag_offset), sc, NEG_BIG)
        m_old = m_ref[h, :, cols]
        m_new = jnp.maximum(m_old, jnp.max(sc, axis=0, keepdims=True))
        a = jnp.exp2(m_old - m_new)
        p = jnp.exp2(sc - m_new).astype(BF16)
        acc_ref[h, :, cols] = a * acc_ref[h, :, cols] + _dot(vaug_ref[h, t], p)
        m_ref[h, :, cols] = m_new

    depth = sc_ref.shape[0]

    def run(steps, ready, after):
        queue = list(ready)
        todo = list(steps[depth:]) + list(after)
        for h, cc, t, off in steps:
            if todo:
                queue.append(scores(*todo.pop(0)[:3]))
            fold(h, cc, t, queue.pop(0), off)
        return queue

    def head_steps(t):
        return [(0, cc, t, None) for cc in range(depth)]

    for d, step in enumerate(head_steps(0)):
        sc_ref[d] = scores(*step[:3])

    def below_diagonal(u, carry):
        tiles = [n_sub * u + c for c in range(n_sub)]
        steps = [(h, cc, t, None) for t in tiles for h in range(2) for cc in range(n_chunks)]
        ahead = run(steps, [sc_ref[d] for d in range(depth)], head_steps(n_sub * (u + 1)))
        for d in range(depth):
            sc_ref[d] = ahead[d]
        return carry

    first = n_sub * i
    lax.fori_loop(0, i, below_diagonal, 0)

    steps = []
    for c in range(n_sub):
        for h in range(2):
            for cc in range(c * tk // cw, n_chunks):
                off = cc * cw - c * tk
                steps.append((h, cc, first + c, off if off < tk else None))
    assert [s[:2] for s in steps[:depth]] == [s[:2] for s in head_steps(first)]
    run(steps, [sc_ref[d] for d in range(depth)], [])

    acc0, acc1 = acc_ref[0], acc_ref[1]
    inv0 = 1.0 / acc0[HEAD_DIM:HEAD_DIM + 1]
    inv1 = 1.0 / acc1[0:1]
    out_t = jnp.concatenate([acc0[0:HEAD_DIM] * inv0, acc1[DEN_ROWS:] * inv1], axis=0)
    o_ref[0] = out_t.T.astype(BF16)


def _fox(qt, fqt, k, fk, vt, tq):
    B, n_tiles, tok_w, tk = qt.shape
    S = n_tiles * tk
    n_pairs = tok_w // LANES
    n_sub = tq // tk
    return pl.pallas_call(
        _fox_kernel,
        grid=(B, n_pairs, S // tq),
        in_specs=[pl.BlockSpec((1, n_sub, LANES, tk), lambda b, j, i: (b, i, j, 0)),
                  pl.BlockSpec((1, n_sub, LANES, tk), lambda b, j, i: (b, i, 0, 0)),
                  pl.BlockSpec((1, S, LANES), lambda b, j, i: (b, 0, j)),
                  pl.BlockSpec((1, S, LANES), lambda b, j, i: (b, 0, 0)),
                  pl.BlockSpec((1, n_tiles, LANES, tk), lambda b, j, i: (b, 0, j, 0))],
        out_specs=pl.BlockSpec((1, tq, LANES), lambda b, j, i: (b, i, j)),
        out_shape=jax.ShapeDtypeStruct((B, S, tok_w), BF16),
        scratch_shapes=[pltpu.VMEM((2, S, 2 * LANES), BF16),
                        pltpu.VMEM((2, n_tiles, HEAD_DIM + DEN_ROWS, tk), BF16),
                        pltpu.VMEM((2 * LANES, tq), BF16),
                        pltpu.VMEM((2, tk, 2 * MXU_N), F32),
                        pltpu.VMEM((2, 1, tq), F32),
                        pltpu.VMEM((2, HEAD_DIM + DEN_ROWS, tq), F32)],
        compiler_params=_params(3),
        name="fox",
    )(qt, fqt, k, fk, vt)


def _mix_b_kernel(x_ref, tok_ref, qm_ref, kb_ref, vb_ref, ones_ref, w_tok_ref, w_mem_ref, g_ref, b_ref, o_ref):
    m_len = kb_ref.shape[1] // MEM_HEADS
    tok_mix = _dot(tok_ref[0], w_tok_ref[...])
    mem_out = _memory_attention(qm_ref[0], kb_ref[0], vb_ref[0], ones_ref[...], m_len)
    mix = tok_mix + _dot(mem_out.astype(BF16), w_mem_ref[...])
    o_ref[0] = _layer_norm(DN_ALPHA * x_ref[0] + mix, g_ref[...], b_ref[...])


def _mix_b(x, tok, qm, k_big, v_big, head_ones, w_tok, w_mem, g, b, ts):
    B, S, D = x.shape
    tok_w = tok.shape[2]
    mm = k_big.shape[1]
    return pl.pallas_call(
        _mix_b_kernel,
        grid=(B, S // ts),
        in_specs=[pl.BlockSpec((1, ts, D), lambda bi, si: (bi, si, 0)),
                  pl.BlockSpec((1, ts, tok_w), lambda bi, si: (bi, si, 0)),
                  pl.BlockSpec((1, ts, MEM_WIDTH), lambda bi, si: (bi, si, 0)),
                  pl.BlockSpec((1, mm, MEM_WIDTH), lambda bi, si: (bi, 0, 0)),
                  pl.BlockSpec((1, mm, MEM_WIDTH), lambda bi, si: (bi, 0, 0)),
                  _const_spec(head_ones.shape), _const_spec(w_tok.shape), _const_spec(w_mem.shape),
                  _const_spec((1, D)), _const_spec((1, D))],
        out_specs=pl.BlockSpec((1, ts, D), lambda bi, si: (bi, si, 0)),
        out_shape=jax.ShapeDtypeStruct((B, S, D), F32),
        compiler_params=_params(2),
        name="mix_b",
    )(x, tok, qm, k_big, v_big, head_ones, w_tok, w_mem, g, b)


def _prep_ffn(w_up, conv_w, conv_b, w_down, fc):
    f = w_up.shape[1] // 2
    fp = _round_up(f, fc)
    pad_cols = lambda a: jnp.pad(a, ((0, 0), (0, fp - f)))
    wu = pad_cols(w_up[:, :f]).astype(BF16)
    wg = pad_cols(w_up[:, f:]).astype(BF16)
    cp = pad_cols(jnp.concatenate([conv_w[:, :f], conv_w[:, f:], conv_b[None, :f], conv_b[None, f:]],
                                  axis=0))
    wd = jnp.pad(w_down, ((0, fp - f), (0, 0))).astype(BF16)
    return wu, wg, cp, wd


def _prep_mix_a(w_in, pool_w, pool_scale, w_out):
    d = w_in.shape[0]
    grp = pool_w.shape[1]
    tok_w = N_POOL * grp
    pad = POOL_PAD - grp
    w_tok = jnp.pad(w_in[:, :tok_w].reshape(d, N_POOL, grp), ((0, 0), (0, 0), (0, pad)))
    w_in_pad = jnp.concatenate([w_tok.reshape(d, N_POOL * POOL_PAD), w_in[:, tok_w:]], axis=1)
    pool_w_pad = jnp.pad(pool_w, ((0, 0), (0, pad), (0, pad)))
    pool_s_pad = jnp.pad(pool_scale.reshape(N_POOL, grp), ((0, 0), (0, pad)))
    o_tok = jnp.pad(w_out[:tok_w].reshape(N_POOL, grp, d), ((0, 0), (0, pad), (0, 0)))
    w_out_pad = jnp.concatenate([o_tok.reshape(N_POOL * POOL_PAD, d), w_out[tok_w:]], axis=0)
    return w_in_pad.astype(BF16), pool_w_pad.astype(BF16), pool_s_pad, w_out_pad.astype(BF16)


def _prep_qkv(w_q, kv_w, f_b, tok_w, ts):
    n_heads = tok_w // HEAD_DIM
    assert n_heads * DECAY_LANES <= LANES
    w_f = jnp.pad(kv_w[:, 2 * tok_w:], ((0, 0), (0, LANES - n_heads)))
    w_nat = jnp.concatenate([w_q[:, tok_w:], kv_w[:, :tok_w]], axis=1).astype(BF16)
    w_tr = jnp.concatenate([w_q[:, :tok_w] * (ATTN_SCALE * LOG2E), kv_w[:, tok_w:2 * tok_w], w_f],
                           axis=1).T.astype(BF16)
    fb_col = jnp.pad(f_b, (0, LANES - n_heads)).reshape(LANES, 1)
    tri = (jnp.arange(ts)[:, None] <= jnp.arange(ts)[None, :]).astype(BF16)
    src = jnp.arange(DECAY_PARTS * LANES)
    part, head = src // LANES, src % LANES
    dst = jnp.arange(LANES)
    valid = (head < n_heads)[:, None]
    pq = (valid & (dst[None, :] == (head * DECAY_LANES + part)[:, None])).astype(BF16)
    pk = -(valid & (dst[None, :] == (head * DECAY_LANES + DECAY_PARTS + part)[:, None])).astype(BF16)
    in_use = dst < n_heads * DECAY_LANES
    cq_col = (in_use & (dst % DECAY_LANES >= DECAY_PARTS)).astype(F32).reshape(LANES, 1)
    ck = (in_use & (dst % DECAY_LANES < DECAY_PARTS)).astype(F32).reshape(1, LANES)
    return w_nat, w_tr, fb_col, tri, pq.T, pk, cq_col, ck


ROW_TILE = 512
FFN_CHUNK = 256
FFN_ROW_TILE = 512
MIX_ROW_TILE = 1024
FFN_SUBTILES = 1
ATTN_TILE = 4096


def kernel(x, mem, a_w_in, a_pool_w, a_pool_scale, a_w_out, b_w_q, b_w_out, kv_w, f_b, mem_w_kv,
           ln1_g, ln1_b, ln2_g, ln2_b, ffn_w_up, ffn_conv_w, ffn_conv_b, ffn_w_down):
    B, S, D = x.shape
    n_a = a_w_in.shape[0]
    n_b = b_w_q.shape[0]
    tok_w = D - MEM_WIDTH
    ts = min(ROW_TILE, S)
    tq = min(ATTN_TILE, S)
    assert S % ts == 0 and S % tq == 0 and tq % ts == 0 and tok_w % LANES == 0

    k_big, v_big = _memkv(mem, mem_w_kv.astype(BF16))
    m_len = mem.shape[1]
    head_ones = (jnp.arange(MEM_HEADS * m_len)[:, None] // m_len
                 == jnp.arange(MEM_WIDTH)[None, :] // HEAD_DIM).astype(BF16)
    row = lambda a, l: a[l].reshape(1, D)

    shared = None
    for l in range(n_a + n_b):
        if l < n_a:
            w_in_pad, pool_w_pad, pool_s_pad, w_out_pad = _prep_mix_a(
                a_w_in[l], a_pool_w[l], a_pool_scale[l], a_w_out[l])
            x = _mix_a(x, w_in_pad, pool_w_pad, pool_s_pad, k_big[l], v_big[l], head_ones, w_out_pad,
                       row(ln1_g, l), row(ln1_b, l), min(MIX_ROW_TILE, S))
        else:
            j = l - n_a
            qt, qm, k_new, vt_new, fqt_new, fk_new = _qkv(
                x, *_prep_qkv(b_w_q[j], kv_w, f_b, tok_w, ts), tok_w, ts)
            if j == 0:
                shared = (fqt_new, k_new, fk_new, vt_new)
            fqt, k_sh, fk_sh, vt_sh = shared
            tok = _fox(qt, fqt, k_sh, fk_sh, vt_sh, tq)
            w_out = b_w_out[j].astype(BF16)
            x = _mix_b(x, tok, qm, k_big[l], v_big[l], head_ones, w_out[:tok_w], w_out[tok_w:],
                       row(ln1_g, l), row(ln1_b, l), min(MIX_ROW_TILE, S))
        wu, wg, cp, wd = _prep_ffn(ffn_w_up[l], ffn_conv_w[l], ffn_conv_b[l], ffn_w_down[l], FFN_CHUNK)
        x = _ffn(x, wu, wg, cp, wd, row(ln2_g, l), row(ln2_b, l), min(FFN_ROW_TILE, S), FFN_CHUNK,
                 FFN_SUBTILES)
    return x
```

```python
import functools

import jax
import jax.numpy as jnp
from jax import lax
from jax.experimental import pallas as pl
from jax.experimental.pallas import tpu as pltpu

F32 = jnp.float32
BF16 = jnp.bfloat16

HEAD_DIM = 64
MEM_HEADS = 4
MEM_WIDTH = MEM_HEADS * HEAD_DIM
POOL_WINDOWS = (2, 4, 8, 16)
N_POOL = len(POOL_WINDOWS)
CONV_WIDTH = 3
LN_EPS = 1e-5
DEPTH = 2
DN_ALPHA = (2.0 * DEPTH) ** 0.25
ATTN_SCALE = HEAD_DIM ** -0.5
LOG2E = 1.4426950408889634

LANES = 128
SUBLANES = 8
MXU_N = 256
VMEM_LIMIT_BYTES = 56 * 1024 * 1024

POOL_PAD = MXU_N
POOL_HALO = 24
DECAY_PARTS = 3
DECAY_LANES = 2 * DECAY_PARTS
DEN_ROWS = 16
NEG_BIG = -1e30


def _round_up(n, m):
    return (n + m - 1) // m * m


def _const_spec(shape):
    nd = len(shape)
    return pl.BlockSpec(shape, lambda *_: (0,) * nd, pipeline_mode=pl.Buffered(1))


def _params(n_grid):
    return pltpu.CompilerParams(dimension_semantics=("arbitrary",) * n_grid,
                                vmem_limit_bytes=VMEM_LIMIT_BYTES)


def _layer_norm(y, g, b):
    mu = jnp.mean(y, axis=-1, keepdims=True)
    d = y - mu
    var = jnp.mean(d * d, axis=-1, keepdims=True)
    return d * lax.rsqrt(var + LN_EPS) * g + b


def _dot(a, b):
    return jnp.dot(a, b, preferred_element_type=F32)


def _dot_nt(a, b):
    return lax.dot_general(a, b, (((1,), (1,)), ((), ())), preferred_element_type=F32)


def _split3(x):
    hi = x.astype(BF16)
    r = x - hi.astype(F32)
    mid = r.astype(BF16)
    lo = (r - mid.astype(F32)).astype(BF16)
    return hi, mid, lo


def _memkv_kernel(mem_ref, w_ref, k_ref, v_ref):
    m_len = mem_ref.shape[1]
    kv = _dot(mem_ref[0].astype(BF16), w_ref[0])
    k = kv[:, :MEM_WIDTH] * ATTN_SCALE
    v = kv[:, MEM_WIDTH:]
    head_of_lane = lax.broadcasted_iota(jnp.int32, (m_len, MEM_WIDTH), 1) // HEAD_DIM
    for h in range(MEM_HEADS):
        sel = head_of_lane == h
        k_ref[0, 0, h * m_len:(h + 1) * m_len, :] = jnp.where(sel, k, 0.0).astype(BF16)
        v_ref[0, 0, h * m_len:(h + 1) * m_len, :] = jnp.where(sel, v, 0.0).astype(BF16)


def _memkv(mem, w_kv_bf16):
    B, M, D = mem.shape
    L = w_kv_bf16.shape[0]
    out = jax.ShapeDtypeStruct((L, B, MEM_HEADS * M, MEM_WIDTH), BF16)
    out_spec = pl.BlockSpec((1, 1, MEM_HEADS * M, MEM_WIDTH), lambda l, b: (l, b, 0, 0))
    return pl.pallas_call(
        _memkv_kernel,
        grid=(L, B),
        in_specs=[pl.BlockSpec((1, M, D), lambda l, b: (b, 0, 0)),
                  pl.BlockSpec((1, D, 2 * MEM_WIDTH), lambda l, b: (l, 0, 0))],
        out_specs=[out_spec, out_spec],
        out_shape=[out, out],
        compiler_params=_params(2),
        name="memkv",
    )(mem, w_kv_bf16)


def _memory_attention(q_mem_bf16, k_big, v_big, head_ones, m_len):
    logits = _dot_nt(q_mem_bf16, k_big)
    weights = []
    for h in range(MEM_HEADS):
        lh = logits[:, h * m_len:(h + 1) * m_len]
        weights.append(jnp.exp(lh - jnp.max(lh, axis=-1, keepdims=True)).astype(BF16))
    e = jnp.concatenate(weights, axis=1)
    return _dot(e, v_big) / _dot(e, head_ones)


def _mix_a_kernel(x_ref, w_in_ref, pool_w_ref, pool_s_ref, kb_ref, vb_ref, ones_ref, w_out_ref, g_ref, b_ref,
                  o_ref, hist_ref, lvl_ref, cat_ref):
    ts = x_ref.shape[1]
    m_len = kb_ref.shape[1] // MEM_HEADS
    tok_w = N_POOL * POOL_PAD
    s = pl.program_id(1)

    @pl.when(s == 0)
    def _():
        hist_ref[0:POOL_HALO, :] = jnp.zeros((POOL_HALO, tok_w), F32)
        lvl_ref[:, 0:SUBLANES, :] = jnp.zeros((2, SUBLANES, POOL_PAD), F32)

    @pl.when(s > 0)
    def _():
        hist_ref[0:POOL_HALO, :] = hist_ref[ts:ts + POOL_HALO, :]

    def window_sum(cols, w, lo, hi):
        base = lo - (POOL_HALO - SUBLANES)
        read = lambda a, b: hist_ref[a:b, cols]
        k = 1
        while 2 * k < w:
            slot = (k.bit_length() - 1) % 2
            lvl_ref[slot, base:hi, :] = read(base, hi) + read(base - k, hi - k)
            read = lambda a, b, slot=slot: lvl_ref[slot, a:b, :]
            k *= 2
        return read(lo, hi) + read(lo - k, hi - k)

    n_parts = MIX_PARTS if ts % (MIX_PARTS * SUBLANES) == 0 else 1
    pr = ts // n_parts
    q_mem = []
    for k in range(n_parts):
        xk = x_ref[0, k * pr:(k + 1) * pr, :]
        proj = _dot(xk.astype(BF16), w_in_ref[...])
        hist_ref[POOL_HALO + k * pr:POOL_HALO + (k + 1) * pr, :] = proj[:, :tok_w]
        q_mem.append(proj[:, tok_w:].astype(BF16))

    for k in range(n_parts):
        rows = slice(k * pr, (k + 1) * pr)
        lo, hi = POOL_HALO + k * pr, POOL_HALO + (k + 1) * pr
        pos = s * ts + k * pr + lax.broadcasted_iota(jnp.int32, (pr, 1), 0)
        for i, w in enumerate(POOL_WINDOWS):
            cols = slice(i * POOL_PAD, (i + 1) * POOL_PAD)
            u = hist_ref[lo:hi, cols]
            inv_count = 1.0 / jnp.minimum(pos + 1, w).astype(F32)
            pooled = window_sum(cols, w, lo, hi) * inv_count - u
            mixed = _dot(pooled.astype(BF16), pool_w_ref[i]) * pool_s_ref[i:i + 1, :]
            cat_ref[rows, cols] = mixed.astype(BF16)

        mem_out = _memory_attention(q_mem[k], kb_ref[0], vb_ref[0], ones_ref[...], m_len)
        cat_ref[rows, tok_w:] = mem_out.astype(BF16)

        mix = _dot(cat_ref[rows, :], w_out_ref[...])
        o_ref[0, rows, :] = _layer_norm(DN_ALPHA * x_ref[0, rows, :] + mix, g_ref[...], b_ref[...])


def _mix_a(x, w_in_pad, pool_w_pad, pool_s_pad, k_big, v_big, head_ones, w_out_pad, g, b, ts):
    B, S, D = x.shape
    wp = w_in_pad.shape[1]
    tok_w = N_POOL * POOL_PAD
    mm = k_big.shape[1]
    return pl.pallas_call(
        _mix_a_kernel,
        grid=(B, S // ts),
        in_specs=[pl.BlockSpec((1, ts, D), lambda bi, si: (bi, si, 0)),
                  _const_spec((D, wp)),
                  _const_spec((N_POOL, POOL_PAD, POOL_PAD)),
                  _const_spec((N_POOL, POOL_PAD)),
                  pl.BlockSpec((1, mm, MEM_WIDTH), lambda bi, si: (bi, 0, 0)),
                  pl.BlockSpec((1, mm, MEM_WIDTH), lambda bi, si: (bi, 0, 0)),
                  _const_spec(head_ones.shape),
                  _const_spec((wp, D)),
                  _const_spec((1, D)),
                  _const_spec((1, D))],
        out_specs=pl.BlockSpec((1, ts, D), lambda bi, si: (bi, si, 0)),
        out_shape=jax.ShapeDtypeStruct((B, S, D), F32),
        scratch_shapes=[pltpu.VMEM((POOL_HALO + ts, tok_w), F32),
                        pltpu.VMEM((2, POOL_HALO + ts, POOL_PAD), F32),
                        pltpu.VMEM((ts, wp), BF16)],
        compiler_params=_params(2),
        name="mix_a",
    )(x, w_in_pad, pool_w_pad, pool_s_pad, k_big, v_big, head_ones, w_out_pad, g, b)


def _ffn_kernel(fc, n_sub, x_ref, wu_ref, wg_ref, cp_ref, wd_ref, g_ref, b_ref, o_ref,
                hu_ref, hg_ref, carry_ref, act_ref):
    rows = x_ref.shape[1] // n_sub
    n_chunks = wu_ref.shape[1] // fc
    halo = SUBLANES
    s = pl.program_id(1)

    @pl.when(s == 0)
    def _():
        carry_ref[...] = jnp.zeros(carry_ref.shape, F32)

    def conv(h_ref, slot, taps, bias):
        return (h_ref[slot, halo - 2:halo - 2 + rows, :] * taps[0:1]
                + h_ref[slot, halo - 1:halo - 1 + rows, :] * taps[1:2]
                + h_ref[slot, halo:halo + rows, :] * taps[2:3] + bias)

    for k in range(n_sub):
        x = x_ref[0, k * rows:(k + 1) * rows, :]
        xb = x.astype(BF16)
        for c in range(n_chunks):
            cols = slice(c * fc, (c + 1) * fc)
            slot = (k * n_chunks + c) % 2
            hu_ref[slot, 0:halo, :] = carry_ref[0, :, cols]
            hg_ref[slot, 0:halo, :] = carry_ref[1, :, cols]
            hu_ref[slot, halo:halo + rows, :] = _dot(xb, wu_ref[:, cols])
            hg_ref[slot, halo:halo + rows, :] = _dot(xb, wg_ref[:, cols])
            carry_ref[0, :, cols] = hu_ref[slot, rows:rows + halo, :]
            carry_ref[1, :, cols] = hg_ref[slot, rows:rows + halo, :]
            cp = cp_ref[:, cols]
            u = conv(hu_ref, slot, cp[0:3], cp[6:7])
            gate = conv(hg_ref, slot, cp[3:6], cp[7:8])
            half = 0.5 * gate
            act_ref[k, :, cols] = ((half + half * jnp.tanh(half)) * u).astype(BF16)
        ffn = _dot(act_ref[k], wd_ref[...])
        o_ref[0, k * rows:(k + 1) * rows, :] = _layer_norm(DN_ALPHA * x + ffn, g_ref[...], b_ref[...])


def _ffn(x, wu, wg, cp, wd, g, b, ts, fc, n_sub):
    B, S, D = x.shape
    fp = wu.shape[1]
    rows = ts // n_sub
    assert rows * n_sub == ts and rows % SUBLANES == 0
    return pl.pallas_call(
        functools.partial(_ffn_kernel, fc, n_sub),
        grid=(B, S // ts),
        in_specs=[pl.BlockSpec((1, ts, D), lambda bi, si: (bi, si, 0)),
                  _const_spec(wu.shape),
                  _const_spec(wg.shape),
                  _const_spec(cp.shape),
                  _const_spec(wd.shape),
                  _const_spec((1, D)),
                  _const_spec((1, D))],
        out_specs=pl.BlockSpec((1, ts, D), lambda bi, si: (bi, si, 0)),
        out_shape=jax.ShapeDtypeStruct((B, S, D), F32),
        scratch_shapes=[pltpu.VMEM((2, SUBLANES + rows, fc), F32),
                        pltpu.VMEM((2, SUBLANES + rows, fc), F32),
                        pltpu.VMEM((2, SUBLANES, fp), F32),
                        pltpu.VMEM((n_sub, rows, fp), BF16)],
        compiler_params=_params(2),
        name="ffn",
    )(x, wu, wg, cp, wd, g, b)


def _qkv_kernel(x_ref, w_nat_ref, w_tr_ref, fb_ref, tri_ref, pqt_ref, pk_ref, cq_ref, ck_ref,
                qt_ref, qm_ref, k_ref, vt_ref, fqt_ref, fk_ref, carry_ref):
    ts = x_ref.shape[1]
    tok_w = k_ref.shape[2]
    s = pl.program_id(1)
    xb = x_ref[0].astype(BF16)
    nat = _dot(xb, w_nat_ref[...])
    qm_ref[0] = nat[:, :MEM_WIDTH].astype(BF16)
    k_ref[0] = nat[:, MEM_WIDTH:].astype(BF16)
    tr = _dot_nt(w_tr_ref[...], xb)
    qt_ref[0, 0] = tr[0:tok_w].astype(BF16)
    vt_ref[0, 0] = tr[tok_w:2 * tok_w].astype(BF16)

    @pl.when(s == 0)
    def _():
        carry_ref[...] = jnp.zeros(carry_ref.shape, F32)

    z = tr[2 * tok_w:] + fb_ref[...]
    log_f = jnp.minimum(z, 0.0) - jnp.log1p(jnp.exp(-jnp.abs(z)))
    part = _dot(jnp.concatenate(_split3(log_f), axis=0), tri_ref[...])
    cum_t = (part[0:LANES] + part[LANES:2 * LANES] + part[2 * LANES:]) + carry_ref[:, 0:1]
    carry_ref[:, 0:1] = cum_t[:, ts - 1:ts]
    dec_t = cum_t * LOG2E
    pieces_t = jnp.concatenate(_split3(dec_t), axis=0)
    fqt_ref[0, 0] = (_dot(pqt_ref[...], pieces_t) + cq_ref[...]).astype(BF16)
    pieces = jnp.concatenate(_split3(dec_t.T), axis=1)
    fk_ref[0] = (_dot(pieces, pk_ref[...]) + ck_ref[...]).astype(BF16)


def _qkv(x, w_nat, w_tr, fb_col, tri, pqt, pk, cq_col, ck, tok_w, ts):
    B, S, D = x.shape
    n_tiles = S // ts
    row = lambda width: pl.BlockSpec((1, ts, width), lambda bi, si: (bi, si, 0))
    row_shape = lambda width: jax.ShapeDtypeStruct((B, S, width), BF16)
    col = lambda height: pl.BlockSpec((1, 1, height, ts), lambda bi, si: (bi, si, 0, 0))
    col_shape = lambda height: jax.ShapeDtypeStruct((B, n_tiles, height, ts), BF16)
    return pl.pallas_call(
        _qkv_kernel,
        grid=(B, n_tiles),
        in_specs=[pl.BlockSpec((1, ts, D), lambda bi, si: (bi, si, 0)),
                  _const_spec(w_nat.shape), _const_spec(w_tr.shape), _const_spec(fb_col.shape),
                  _const_spec(tri.shape), _const_spec(pqt.shape), _const_spec(pk.shape),
                  _const_spec(cq_col.shape), _const_spec(ck.shape)],
        out_specs=[col(tok_w), row(MEM_WIDTH), row(tok_w), col(tok_w), col(LANES), row(LANES)],
        out_shape=[col_shape(tok_w), row_shape(MEM_WIDTH), row_shape(tok_w), col_shape(tok_w),
                   col_shape(LANES), row_shape(LANES)],
        scratch_shapes=[pltpu.VMEM((LANES, LANES), F32)],
        compiler_params=_params(2),
        name="qkv",
    )(x, w_nat, w_tr, fb_col, tri, pqt, pk, cq_col, ck)


def _fox_kernel(qt_ref, fqt_ref, k_ref, fk_ref, vt_ref, o_ref,
                kaug_ref, vaug_ref, qat_ref, sc_ref, m_ref, acc_ref):
    n_sub, tk = qt_ref.shape[1], qt_ref.shape[3]
    tq = n_sub * tk
    seq = k_ref.shape[1]
    j = pl.program_id(1)
    i = pl.program_id(2)

    @pl.when(i == 0)
    def _():
        lane = lax.broadcasted_iota(jnp.int32, (seq, LANES), 1)
        kp = k_ref[0]
        fk = fk_ref[0]
        zero = jnp.zeros((seq, LANES), BF16)
        vt = vt_ref[0]
        feat = lax.broadcasted_iota(jnp.int32, vt.shape, 1)
        one = jnp.ones(vt.shape, BF16)
        for h in range(2):
            head = 2 * j + h
            kaug_ref[h, :, 0:LANES] = jnp.where(lane // HEAD_DIM == h, kp, zero)
            in_head = (lane >= head * DECAY_LANES) & (lane < (head + 1) * DECAY_LANES)
            kaug_ref[h, :, LANES:] = jnp.where(in_head, fk, zero)
            own = jnp.where(feat // HEAD_DIM == h, vt, one)
            lo = h * (HEAD_DIM - DEN_ROWS)
            vaug_ref[h] = own[:, lo:lo + HEAD_DIM + DEN_ROWS, :]

    for c in range(n_sub):
        qat_ref[0:LANES, c * tk:(c + 1) * tk] = qt_ref[0, c]
        qat_ref[LANES:, c * tk:(c + 1) * tk] = fqt_ref[0, c]
    m_ref[...] = jnp.full(m_ref.shape, NEG_BIG, F32)
    acc_ref[...] = jnp.zeros(acc_ref.shape, F32)

    cw = sc_ref.shape[2]
    n_chunks = tq // cw

    def scores(h, cc, t):
        return _dot(kaug_ref[h, pl.ds(pl.multiple_of(t * tk, tk), tk), :],
                    qat_ref[:, cc * cw:(cc + 1) * cw])

    masks = {}

    def visible(diag_offset):
        if diag_offset not in masks:
            key = lax.broadcasted_iota(jnp.int32, (tk, cw), 0)
            qry = lax.broadcasted_iota(jnp.int32, (tk, cw), 1)
            masks[diag_offset] = qry + diag_offset >= key
        return masks[diag_offset]

    def fold(h, cc, t, sc, diag_offset):
        cols = slice(cc * cw, (cc + 1) * cw)
        if diag_offset is not None:
            sc = jnp.where(visible(diag_offset), sc, NEG_BIG)
        m_old = m_ref[h, :, cols]
        m_new = jnp.maximum(m_old, jnp.max(sc, axis=0, keepdims=True))
        a = jnp.exp2(m_old - m_new)
        p = jnp.exp2(sc - m_new).astype(BF16)
        acc_ref[h, :, cols] = a * acc_ref[h, :, cols] + _dot(vaug_ref[h, t], p)
        m_ref[h, :, cols] = m_new

    depth = sc_ref.shape[0]

    def run(steps, ready, after):
        queue = list(ready)
        todo = list(steps[depth:]) + list(after)
        for h, cc, t, off in steps:
            if todo:
                queue.append(scores(*todo.pop(0)[:3]))
            fold(h, cc, t, queue.pop(0), off)
        return queue

    def head_steps(t):
        return [(0, cc, t, None) for cc in range(depth)]

    for d, step in enumerate(head_steps(0)):
        sc_ref[d] = scores(*step[:3])

    def below_diagonal(u, carry):
        tiles = [n_sub * u + c for c in range(n_sub)]
        steps = [(h, cc, t, None) for t in tiles for h in range(2) for cc in range(n_chunks)]
        ahead = run(steps, [sc_ref[d] for d in range(depth)], head_steps(n_sub * (u + 1)))
        for d in range(depth):
            sc_ref[d] = ahead[d]
        return carry

    first = n_sub * i
    lax.fori_loop(0, i, below_diagonal, 0)

    steps = []
    for c in range(n_sub):
        for h in range(2):
            for cc in range(c * tk // cw, n_chunks):
                off = cc * cw - c * tk
                steps.append((h, cc, first + c, off if off < tk else None))
    assert [s[:2] for s in steps[:depth]] == [s[:2] for s in head_steps(first)]
    run(steps, [sc_ref[d] for d in range(depth)], [])

    acc0, acc1 = acc_ref[0], acc_ref[1]
    inv0 = 1.0 / acc0[HEAD_DIM:HEAD_DIM + 1]
    inv1 = 1.0 / acc1[0:1]
    out_t = jnp.concatenate([acc0[0:HEAD_DIM] * inv0, acc1[DEN_ROWS:] * inv1], axis=0)
    o_ref[0] = out_t.T.astype(BF16)


def _fox(qt, fqt, k, fk, vt, tq):
    B, n_tiles, tok_w, tk = qt.shape
    S = n_tiles * tk
    n_pairs = tok_w // LANES
    n_sub = tq // tk
    return pl.pallas_call(
        _fox_kernel,
        grid=(B, n_pairs, S // tq),
        in_specs=[pl.BlockSpec((1, n_sub, LANES, tk), lambda b, j, i: (b, i, j, 0)),
                  pl.BlockSpec((1, n_sub, LANES, tk), lambda b, j, i: (b, i, 0, 0)),
                  pl.BlockSpec((1, S, LANES), lambda b, j, i: (b, 0, j)),
                  pl.BlockSpec((1, S, LANES), lambda b, j, i: (b, 0, 0)),
                  pl.BlockSpec((1, n_tiles, LANES, tk), lambda b, j, i: (b, 0, j, 0))],
        out_specs=pl.BlockSpec((1, tq, LANES), lambda b, j, i: (b, i, j)),
        out_shape=jax.ShapeDtypeStruct((B, S, tok_w), BF16),
        scratch_shapes=[pltpu.VMEM((2, S, 2 * LANES), BF16),
                        pltpu.VMEM((2, n_tiles, HEAD_DIM + DEN_ROWS, tk), BF16),
                        pltpu.VMEM((2 * LANES, tq), BF16),
                        pltpu.VMEM((2, tk, 2 * MXU_N), F32),
                        pltpu.VMEM((2, 1, tq), F32),
                        pltpu.VMEM((2, HEAD_DIM + DEN_ROWS, tq), F32)],
        compiler_params=_params(3),
        name="fox",
    )(qt, fqt, k, fk, vt)


def _mix_b_kernel(x_ref, tok_ref, qm_ref, kb_ref, vb_ref, ones_ref, w_tok_ref, w_mem_ref, g_ref, b_ref, o_ref):
    m_len = kb_ref.shape[1] // MEM_HEADS
    ts = x_ref.shape[1]
    n_parts = MIX_PARTS if ts % (MIX_PARTS * SUBLANES) == 0 else 1
    pr = ts // n_parts
    for k in range(n_parts):
        rows = slice(k * pr, (k + 1) * pr)
        mem_out = _memory_attention(qm_ref[0, rows, :], kb_ref[0], vb_ref[0], ones_ref[...], m_len)
        mix = _dot(tok_ref[0, rows, :], w_tok_ref[...]) + _dot(mem_out.astype(BF16), w_mem_ref[...])
        o_ref[0, rows, :] = _layer_norm(DN_ALPHA * x_ref[0, rows, :] + mix, g_ref[...], b_ref[...])


def _mix_b(x, tok, qm, k_big, v_big, head_ones, w_tok, w_mem, g, b, ts):
    B, S, D = x.shape
    tok_w = tok.shape[2]
    mm = k_big.shape[1]
    return pl.pallas_call(
        _mix_b_kernel,
        grid=(B, S // ts),
        in_specs=[pl.BlockSpec((1, ts, D), lambda bi, si: (bi, si, 0)),
                  pl.BlockSpec((1, ts, tok_w), lambda bi, si: (bi, si, 0)),
                  pl.BlockSpec((1, ts, MEM_WIDTH), lambda bi, si: (bi, si, 0)),
                  pl.BlockSpec((1, mm, MEM_WIDTH), lambda bi, si: (bi, 0, 0)),
                  pl.BlockSpec((1, mm, MEM_WIDTH), lambda bi, si: (bi, 0, 0)),
                  _const_spec(head_ones.shape), _const_spec(w_tok.shape), _const_spec(w_mem.shape),
                  _const_spec((1, D)), _const_spec((1, D))],
        out_specs=pl.BlockSpec((1, ts, D), lambda bi, si: (bi, si, 0)),
        out_shape=jax.ShapeDtypeStruct((B, S, D), F32),
        compiler_params=_params(2),
        name="mix_b",
    )(x, tok, qm, k_big, v_big, head_ones, w_tok, w_mem, g, b)


def _prep_ffn(w_up, conv_w, conv_b, w_down, fc):
    f = w_up.shape[1] // 2
    fp = _round_up(f, fc)
    pad_cols = lambda a: jnp.pad(a, ((0, 0), (0, fp - f)))
    wu = pad_cols(w_up[:, :f]).astype(BF16)
    wg = pad_cols(w_up[:, f:]).astype(BF16)
    cp = pad_cols(jnp.concatenate([conv_w[:, :f], conv_w[:, f:], conv_b[None, :f], conv_b[None, f:]],
                                  axis=0))
    wd = jnp.pad(w_down, ((0, fp - f), (0, 0))).astype(BF16)
    return wu, wg, cp, wd


def _prep_mix_a(w_in, pool_w, pool_scale, w_out):
    d = w_in.shape[0]
    grp = pool_w.shape[1]
    tok_w = N_POOL * grp
    pad = POOL_PAD - grp
    w_tok = jnp.pad(w_in[:, :tok_w].reshape(d, N_POOL, grp), ((0, 0), (0, 0), (0, pad)))
    w_in_pad = jnp.concatenate([w_tok.reshape(d, N_POOL * POOL_PAD), w_in[:, tok_w:]], axis=1)
    pool_w_pad = jnp.pad(pool_w, ((0, 0), (0, pad), (0, pad)))
    pool_s_pad = jnp.pad(pool_scale.reshape(N_POOL, grp), ((0, 0), (0, pad)))
    o_tok = jnp.pad(w_out[:tok_w].reshape(N_POOL, grp, d), ((0, 0), (0, pad), (0, 0)))
    w_out_pad = jnp.concatenate([o_tok.reshape(N_POOL * POOL_PAD, d), w_out[tok_w:]], axis=0)
    return w_in_pad.astype(BF16), pool_w_pad.astype(BF16), pool_s_pad, w_out_pad.astype(BF16)


def _prep_qkv(w_q, kv_w, f_b, tok_w, ts):
    n_heads = tok_w // HEAD_DIM
    assert n_heads * DECAY_LANES <= LANES
    w_f = jnp.pad(kv_w[:, 2 * tok_w:], ((0, 0), (0, LANES - n_heads)))
    w_nat = jnp.concatenate([w_q[:, tok_w:], kv_w[:, :tok_w]], axis=1).astype(BF16)
    w_tr = jnp.concatenate([w_q[:, :tok_w] * (ATTN_SCALE * LOG2E), kv_w[:, tok_w:2 * tok_w], w_f],
                           axis=1).T.astype(BF16)
    fb_col = jnp.pad(f_b, (0, LANES - n_heads)).reshape(LANES, 1)
    tri = (jnp.arange(ts)[:, None] <= jnp.arange(ts)[None, :]).astype(BF16)
    src = jnp.arange(DECAY_PARTS * LANES)
    part, head = src // LANES, src % LANES
    dst = jnp.arange(LANES)
    valid = (head < n_heads)[:, None]
    pq = (valid & (dst[None, :] == (head * DECAY_LANES + part)[:, None])).astype(BF16)
    pk = -(valid & (dst[None, :] == (head * DECAY_LANES + DECAY_PARTS + part)[:, None])).astype(BF16)
    in_use = dst < n_heads * DECAY_LANES
    cq_col = (in_use & (dst % DECAY_LANES >= DECAY_PARTS)).astype(F32).reshape(LANES, 1)
    ck = (in_use & (dst % DECAY_LANES < DECAY_PARTS)).astype(F32).reshape(1, LANES)
    return w_nat, w_tr, fb_col, tri, pq.T, pk, cq_col, ck


ROW_TILE = 512
FFN_CHUNK = 256
FFN_ROW_TILE = 512
MIX_ROW_TILE = 1024
MIX_PARTS = 2
FFN_SUBTILES = 1
ATTN_TILE = 4096


def kernel(x, mem, a_w_in, a_pool_w, a_pool_scale, a_w_out, b_w_q, b_w_out, kv_w, f_b, mem_w_kv,
           ln1_g, ln1_b, ln2_g, ln2_b, ffn_w_up, ffn_conv_w, ffn_conv_b, ffn_w_down):
    B, S, D = x.shape
    n_a = a_w_in.shape[0]
    n_b = b_w_q.shape[0]
    tok_w = D - MEM_WIDTH
    ts = min(ROW_TILE, S)
    tq = min(ATTN_TILE, S)
    assert S % ts == 0 and S % tq == 0 and tq % ts == 0 and tok_w % LANES == 0

    k_big, v_big = _memkv(mem, mem_w_kv.astype(BF16))
    m_len = mem.shape[1]
    head_ones = (jnp.arange(MEM_HEADS * m_len)[:, None] // m_len
                 == jnp.arange(MEM_WIDTH)[None, :] // HEAD_DIM).astype(BF16)
    row = lambda a, l: a[l].reshape(1, D)

    shared = None
    for l in range(n_a + n_b):
        if l < n_a:
            w_in_pad, pool_w_pad, pool_s_pad, w_out_pad = _prep_mix_a(
                a_w_in[l], a_pool_w[l], a_pool_scale[l], a_w_out[l])
            x = _mix_a(x, w_in_pad, pool_w_pad, pool_s_pad, k_big[l], v_big[l], head_ones, w_out_pad,
                       row(ln1_g, l), row(ln1_b, l), min(MIX_ROW_TILE, S))
        else:
            j = l - n_a
            qt, qm, k_new, vt_new, fqt_new, fk_new = _qkv(
                x, *_prep_qkv(b_w_q[j], kv_w, f_b, tok_w, ts), tok_w, ts)
            if j == 0:
                shared = (fqt_new, k_new, fk_new, vt_new)
            fqt, k_sh, fk_sh, vt_sh = shared
            tok = _fox(qt, fqt, k_sh, fk_sh, vt_sh, tq)
            w_out = b_w_out[j].astype(BF16)
            x = _mix_b(x, tok, qm, k_big[l], v_big[l], head_ones, w_out[:tok_w], w_out[tok_w:],
                       row(ln1_g, l), row(ln1_b, l), min(MIX_ROW_TILE, S))
        wu, wg, cp, wd = _prep_ffn(ffn_w_up[l], ffn_conv_w[l], ffn_conv_b[l], ffn_w_down[l], FFN_CHUNK)
        x = _ffn(x, wu, wg, cp, wd, row(ln2_g, l), row(ln2_b, l), min(FFN_ROW_TILE, S), FFN_CHUNK,
                 FFN_SUBTILES)
    return x
```

```python
import functools

import jax
import jax.numpy as jnp
from jax import lax
from jax.experimental import pallas as pl
from jax.experimental.pallas import tpu as pltpu

F32 = jnp.float32
BF16 = jnp.bfloat16

HEAD_DIM = 64
MEM_HEADS = 4
MEM_WIDTH = MEM_HEADS * HEAD_DIM
POOL_WINDOWS = (2, 4, 8, 16)
N_POOL = len(POOL_WINDOWS)
CONV_WIDTH = 3
LN_EPS = 1e-5
DEPTH = 2
DN_ALPHA = (2.0 * DEPTH) ** 0.25
ATTN_SCALE = HEAD_DIM ** -0.5
LOG2E = 1.4426950408889634

LANES = 128
SUBLANES = 8
MXU_N = 256
VMEM_LIMIT_BYTES = 56 * 1024 * 1024

POOL_PAD = MXU_N
POOL_HALO = 24
DECAY_PARTS = 3
DECAY_LANES = 2 * DECAY_PARTS
DEN_ROWS = 16
NEG_BIG = -1e30


def _round_up(n, m):
    return (n + m - 1) // m * m


def _const_spec(shape):
    nd = len(shape)
    return pl.BlockSpec(shape, lambda *_: (0,) * nd, pipeline_mode=pl.Buffered(1))


def _params(n_grid):
    return pltpu.CompilerParams(dimension_semantics=("arbitrary",) * n_grid,
                                vmem_limit_bytes=VMEM_LIMIT_BYTES)


def _layer_norm(y, g, b):
    mu = jnp.mean(y, axis=-1, keepdims=True)
    d = y - mu
    var = jnp.mean(d * d, axis=-1, keepdims=True)
    return d * lax.rsqrt(var + LN_EPS) * g + b


def _dot(a, b):
    return jnp.dot(a, b, preferred_element_type=F32)


def _dot_nt(a, b):
    return lax.dot_general(a, b, (((1,), (1,)), ((), ())), preferred_element_type=F32)


def _split3(x):
    hi = x.astype(BF16)
    r = x - hi.astype(F32)
    mid = r.astype(BF16)
    lo = (r - mid.astype(F32)).astype(BF16)
    return hi, mid, lo


def _memkv_kernel(mem_ref, w_ref, k_ref, v_ref):
    m_len = mem_ref.shape[1]
    kv = _dot(mem_ref[0].astype(BF16), w_ref[0])
    k = kv[:, :MEM_WIDTH] * ATTN_SCALE
    v = kv[:, MEM_WIDTH:]
    head_of_lane = lax.broadcasted_iota(jnp.int32, (m_len, MEM_WIDTH), 1) // HEAD_DIM
    for h in range(MEM_HEADS):
        sel = head_of_lane == h
        k_ref[0, 0, h * m_len:(h + 1) * m_len, :] = jnp.where(sel, k, 0.0).astype(BF16)
        v_ref[0, 0, h * m_len:(h + 1) * m_len, :] = jnp.where(sel, v, 0.0).astype(BF16)


def _memkv(mem, w_kv_bf16):
    B, M, D = mem.shape
    L = w_kv_bf16.shape[0]
    out = jax.ShapeDtypeStruct((L, B, MEM_HEADS * M, MEM_WIDTH), BF16)
    out_spec = pl.BlockSpec((1, 1, MEM_HEADS * M, MEM_WIDTH), lambda l, b: (l, b, 0, 0))
    return pl.pallas_call(
        _memkv_kernel,
        grid=(L, B),
        in_specs=[pl.BlockSpec((1, M, D), lambda l, b: (b, 0, 0)),
                  pl.BlockSpec((1, D, 2 * MEM_WIDTH), lambda l, b: (l, 0, 0))],
        out_specs=[out_spec, out_spec],
        out_shape=[out, out],
        compiler_params=_params(2),
        name="memkv",
    )(mem, w_kv_bf16)


def _memory_attention(q_mem_bf16, k_big, v_big, head_ones, m_len):
    logits = _dot_nt(q_mem_bf16, k_big)
    weights = []
    for h in range(MEM_HEADS):
        lh = logits[:, h * m_len:(h + 1) * m_len]
        weights.append(jnp.exp(lh - jnp.max(lh, axis=-1, keepdims=True)).astype(BF16))
    e = jnp.concatenate(weights, axis=1)
    return _dot(e, v_big) / _dot(e, head_ones)


def _mix_a_kernel(x_ref, w_in_ref, pool_w_ref, pool_s_ref, kb_ref, vb_ref, ones_ref, w_out_ref, g_ref, b_ref,
                  o_ref, hist_ref, lvl_ref, cat_ref):
    ts = x_ref.shape[1]
    m_len = kb_ref.shape[1] // MEM_HEADS
    tok_w = N_POOL * POOL_PAD
    s = pl.program_id(1)

    @pl.when(s == 0)
    def _():
        hist_ref[0:POOL_HALO, :] = jnp.zeros((POOL_HALO, tok_w), F32)
        lvl_ref[:, 0:SUBLANES, :] = jnp.zeros((2, SUBLANES, POOL_PAD), F32)

    @pl.when(s > 0)
    def _():
        hist_ref[0:POOL_HALO, :] = hist_ref[ts:ts + POOL_HALO, :]

    def window_sum(cols, w, lo, hi):
        base = lo - (POOL_HALO - SUBLANES)
        read = lambda a, b: hist_ref[a:b, cols]
        k = 1
        while 2 * k < w:
            slot = (k.bit_length() - 1) % 2
            lvl_ref[slot, base:hi, :] = read(base, hi) + read(base - k, hi - k)
            read = lambda a, b, slot=slot: lvl_ref[slot, a:b, :]
            k *= 2
        return read(lo, hi) + read(lo - k, hi - k)

    n_parts = MIX_PARTS if ts % (MIX_PARTS * SUBLANES) == 0 else 1
    pr = ts // n_parts
    q_mem = []
    for k in range(n_parts):
        xk = x_ref[0, k * pr:(k + 1) * pr, :]
        proj = _dot(xk.astype(BF16), w_in_ref[...])
        hist_ref[POOL_HALO + k * pr:POOL_HALO + (k + 1) * pr, :] = proj[:, :tok_w]
        q_mem.append(proj[:, tok_w:].astype(BF16))

    for k in range(n_parts):
        rows = slice(k * pr, (k + 1) * pr)
        lo, hi = POOL_HALO + k * pr, POOL_HALO + (k + 1) * pr
        pos = s * ts + k * pr + lax.broadcasted_iota(jnp.int32, (pr, 1), 0)
        for i, w in enumerate(POOL_WINDOWS):
            cols = slice(i * POOL_PAD, (i + 1) * POOL_PAD)
            u = hist_ref[lo:hi, cols]
            inv_count = 1.0 / jnp.minimum(pos + 1, w).astype(F32)
            pooled = window_sum(cols, w, lo, hi) * inv_count - u
            mixed = _dot(pooled.astype(BF16), pool_w_ref[i]) * pool_s_ref[i:i + 1, :]
            cat_ref[rows, cols] = mixed.astype(BF16)

        mem_out = _memory_attention(q_mem[k], kb_ref[0], vb_ref[0], ones_ref[...], m_len)
        cat_ref[rows, tok_w:] = mem_out.astype(BF16)

        mix = _dot(cat_ref[rows, :], w_out_ref[...])
        o_ref[0, rows, :] = _layer_norm(DN_ALPHA * x_ref[0, rows, :] + mix, g_ref[...], b_ref[...])


def _mix_a(x, w_in_pad, pool_w_pad, pool_s_pad, k_big, v_big, head_ones, w_out_pad, g, b, ts):
    B, S, D = x.shape
    wp = w_in_pad.shape[1]
    tok_w = N_POOL * POOL_PAD
    mm = k_big.shape[1]
    return pl.pallas_call(
        _mix_a_kernel,
        grid=(B, S // ts),
        in_specs=[pl.BlockSpec((1, ts, D), lambda bi, si: (bi, si, 0)),
                  _const_spec((D, wp)),
                  _const_spec((N_POOL, POOL_PAD, POOL_PAD)),
                  _const_spec((N_POOL, POOL_PAD)),
                  pl.BlockSpec((1, mm, MEM_WIDTH), lambda bi, si: (bi, 0, 0)),
                  pl.BlockSpec((1, mm, MEM_WIDTH), lambda bi, si: (bi, 0, 0)),
                  _const_spec(head_ones.shape),
                  _const_spec((wp, D)),
                  _const_spec((1, D)),
                  _const_spec((1, D))],
        out_specs=pl.BlockSpec((1, ts, D), lambda bi, si: (bi, si, 0)),
        out_shape=jax.ShapeDtypeStruct((B, S, D), F32),
        scratch_shapes=[pltpu.VMEM((POOL_HALO + ts, tok_w), F32),
                        pltpu.VMEM((2, POOL_HALO + ts, POOL_PAD), F32),
                        pltpu.VMEM((ts, wp), BF16)],
        compiler_params=_params(2),
        name="mix_a",
    )(x, w_in_pad, pool_w_pad, pool_s_pad, k_big, v_big, head_ones, w_out_pad, g, b)


def _ffn_kernel(fc, n_sub, x_ref, wu_ref, wg_ref, cp_ref, wd_ref, g_ref, b_ref, o_ref,
                hu_ref, hg_ref, carry_ref, act_ref):
    rows = x_ref.shape[1] // n_sub
    n_chunks = wu_ref.shape[1] // fc
    halo = SUBLANES
    s = pl.program_id(1)

    @pl.when(s == 0)
    def _():
        carry_ref[...] = jnp.zeros(carry_ref.shape, F32)

    def conv(h_ref, slot, taps, bias):
        return (h_ref[slot, halo - 2:halo - 2 + rows, :] * taps[0:1]
                + h_ref[slot, halo - 1:halo - 1 + rows, :] * taps[1:2]
                + h_ref[slot, halo:halo + rows, :] * taps[2:3] + bias)

    for k in range(n_sub):
        x = x_ref[0, k * rows:(k + 1) * rows, :]
        xb = x.astype(BF16)
        for c in range(n_chunks):
            cols = slice(c * fc, (c + 1) * fc)
            slot = (k * n_chunks + c) % 2
            hu_ref[slot, 0:halo, :] = carry_ref[0, :, cols]
            hg_ref[slot, 0:halo, :] = carry_ref[1, :, cols]
            hu_ref[slot, halo:halo + rows, :] = _dot(xb, wu_ref[:, cols])
            hg_ref[slot, halo:halo + rows, :] = _dot(xb, wg_ref[:, cols])
            carry_ref[0, :, cols] = hu_ref[slot, rows:rows + halo, :]
            carry_ref[1, :, cols] = hg_ref[slot, rows:rows + halo, :]
            cp = cp_ref[:, cols]
            u = conv(hu_ref, slot, cp[0:3], cp[6:7])
            gate = conv(hg_ref, slot, cp[3:6], cp[7:8])
            half = 0.5 * gate
            act_ref[k, :, cols] = ((half + half * jnp.tanh(half)) * u).astype(BF16)
        ffn = _dot(act_ref[k], wd_ref[...])
        o_ref[0, k * rows:(k + 1) * rows, :] = _layer_norm(DN_ALPHA * x + ffn, g_ref[...], b_ref[...])


def _ffn(x, wu, wg, cp, wd, g, b, ts, fc, n_sub):
    B, S, D = x.shape
    fp = wu.shape[1]
    rows = ts // n_sub
    assert rows * n_sub == ts and rows % SUBLANES == 0
    return pl.pallas_call(
        functools.partial(_ffn_kernel, fc, n_sub),
        grid=(B, S // ts),
        in_specs=[pl.BlockSpec((1, ts, D), lambda bi, si: (bi, si, 0)),
                  _const_spec(wu.shape),
                  _const_spec(wg.shape),
                  _const_spec(cp.shape),
                  _const_spec(wd.shape),
                  _const_spec((1, D)),
                  _const_spec((1, D))],
        out_specs=pl.BlockSpec((1, ts, D), lambda bi, si: (bi, si, 0)),
        out_shape=jax.ShapeDtypeStruct((B, S, D), F32),
        scratch_shapes=[pltpu.VMEM((2, SUBLANES + rows, fc), F32),
                        pltpu.VMEM((2, SUBLANES + rows, fc), F32),
                        pltpu.VMEM((2, SUBLANES, fp), F32),
                        pltpu.VMEM((n_sub, rows, fp), BF16)],
        compiler_params=_params(2),
        name="ffn",
    )(x, wu, wg, cp, wd, g, b)


def _qkv_kernel(x_ref, w_nat_ref, w_tr_ref, fb_ref, tri_ref, pqt_ref, pk_ref, cq_ref, ck_ref,
                qt_ref, qm_ref, k_ref, vt_ref, fqt_ref, fk_ref, carry_ref):
    ts = x_ref.shape[1]
    tok_w = k_ref.shape[2]
    s = pl.program_id(1)
    xb = x_ref[0].astype(BF16)
    nat = _dot(xb, w_nat_ref[...])
    qm_ref[0] = nat[:, :MEM_WIDTH].astype(BF16)
    k_ref[0] = nat[:, MEM_WIDTH:].astype(BF16)
    tr = _dot_nt(w_tr_ref[...], xb)
    qt_ref[0, 0] = tr[0:tok_w].astype(BF16)
    vt_ref[0, 0] = tr[tok_w:2 * tok_w].astype(BF16)

    @pl.when(s == 0)
    def _():
        carry_ref[...] = jnp.zeros(carry_ref.shape, F32)

    z = tr[2 * tok_w:] + fb_ref[...]
    log_f = jnp.minimum(z, 0.0) - jnp.log1p(jnp.exp(-jnp.abs(z)))
    part = _dot(jnp.concatenate(_split3(log_f), axis=0), tri_ref[...])
    cum_t = (part[0:LANES] + part[LANES:2 * LANES] + part[2 * LANES:]) + carry_ref[:, 0:1]
    carry_ref[:, 0:1] = cum_t[:, ts - 1:ts]
    dec_t = cum_t * LOG2E
    pieces_t = jnp.concatenate(_split3(dec_t), axis=0)
    fqt_ref[0, 0] = (_dot(pqt_ref[...], pieces_t) + cq_ref[...]).astype(BF16)
    pieces = jnp.concatenate(_split3(dec_t.T), axis=1)
    fk_ref[0] = (_dot(pieces, pk_ref[...]) + ck_ref[...]).astype(BF16)


def _qkv(x, w_nat, w_tr, fb_col, tri, pqt, pk, cq_col, ck, tok_w, ts):
    B, S, D = x.shape
    n_tiles = S // ts
    row = lambda width: pl.BlockSpec((1, ts, width), lambda bi, si: (bi, si, 0))
    row_shape = lambda width: jax.ShapeDtypeStruct((B, S, width), BF16)
    col = lambda height: pl.BlockSpec((1, 1, height, ts), lambda bi, si: (bi, si, 0, 0))
    col_shape = lambda height: jax.ShapeDtypeStruct((B, n_tiles, height, ts), BF16)
    return pl.pallas_call(
        _qkv_kernel,
        grid=(B, n_tiles),
        in_specs=[pl.BlockSpec((1, ts, D), lambda bi, si: (bi, si, 0)),
                  _const_spec(w_nat.shape), _const_spec(w_tr.shape), _const_spec(fb_col.shape),
                  _const_spec(tri.shape), _const_spec(pqt.shape), _const_spec(pk.shape),
                  _const_spec(cq_col.shape), _const_spec(ck.shape)],
        out_specs=[col(tok_w), row(MEM_WIDTH), row(tok_w), col(tok_w), col(LANES), row(LANES)],
        out_shape=[col_shape(tok_w), row_shape(MEM_WIDTH), row_shape(tok_w), col_shape(tok_w),
                   col_shape(LANES), row_shape(LANES)],
        scratch_shapes=[pltpu.VMEM((LANES, LANES), F32)],
        compiler_params=_params(2),
        name="qkv",
    )(x, w_nat, w_tr, fb_col, tri, pqt, pk, cq_col, ck)


def _fox_kernel(qt_ref, fqt_ref, k_ref, fk_ref, vt_ref, o_ref,
                kaug_ref, vaug_ref, qat_ref, sc_ref, m_ref, acc_ref):
    n_sub, tk = qt_ref.shape[1], qt_ref.shape[3]
    tq = n_sub * tk
    seq = k_ref.shape[1]
    j = pl.program_id(1)
    i = pl.program_id(2)

    @pl.when(i == 0)
    def _():
        lane = lax.broadcasted_iota(jnp.int32, (seq, LANES), 1)
        kp = k_ref[0]
        fk = fk_ref[0]
        zero = jnp.zeros((seq, LANES), BF16)
        vt = vt_ref[0]
        feat = lax.broadcasted_iota(jnp.int32, vt.shape, 1)
        one = jnp.ones(vt.shape, BF16)
        for h in range(2):
            head = 2 * j + h
            kaug_ref[h, :, 0:LANES] = jnp.where(lane // HEAD_DIM == h, kp, zero)
            in_head = (lane >= head * DECAY_LANES) & (lane < (head + 1) * DECAY_LANES)
            kaug_ref[h, :, LANES:] = jnp.where(in_head, fk, zero)
            own = jnp.where(feat // HEAD_DIM == h, vt, one)
            lo = h * (HEAD_DIM - DEN_ROWS)
            vaug_ref[h] = own[:, lo:lo + HEAD_DIM + DEN_ROWS, :]

    for c in range(n_sub):
        qat_ref[0:LANES, c * tk:(c + 1) * tk] = qt_ref[0, c]
        qat_ref[LANES:, c * tk:(c + 1) * tk] = fqt_ref[0, c]
    m_ref[...] = jnp.full(m_ref.shape, NEG_BIG, F32)
    acc_ref[...] = jnp.zeros(acc_ref.shape, F32)

    cw = sc_ref.shape[2]
    n_chunks = tq // cw

    def scores(h, cc, t):
        return _dot(kaug_ref[h, pl.ds(pl.multiple_of(t * tk, tk), tk), :],
                    qat_ref[:, cc * cw:(cc + 1) * cw])

    masks = {}

    def visible(diag_offset):
        if diag_offset not in masks:
            key = lax.broadcasted_iota(jnp.int32, (tk, cw), 0)
            qry = lax.broadcasted_iota(jnp.int32, (tk, cw), 1)
            masks[diag_offset] = qry + diag_offset >= key
        return masks[diag_offset]

    def fold(h, cc, t, sc, diag_offset):
        cols = slice(cc * cw, (cc + 1) * cw)
        if diag_offset is not None:
            sc = jnp.where(visible(diag_offset), sc, NEG_BIG)
        m_old = m_ref[h, :, cols]
        m_new = jnp.maximum(m_old, jnp.max(sc, axis=0, keepdims=True))
        a = jnp.exp2(m_old - m_new)
        p = jnp.exp2(sc - m_new).astype(BF16)
        acc_ref[h, :, cols] = a * acc_ref[h, :, cols] + _dot(vaug_ref[h, t], p)
        m_ref[h, :, cols] = m_new

    depth = sc_ref.shape[0]

    def run(steps, ready, after):
        queue = list(ready)
        todo = list(steps[depth:]) + list(after)
        for h, cc, t, off in steps:
            if todo:
                queue.append(scores(*todo.pop(0)[:3]))
            fold(h, cc, t, queue.pop(0), off)
        return queue

    def head_steps(t):
        return [(0, cc, t, None) for cc in range(depth)]

    for d, step in enumerate(head_steps(0)):
        sc_ref[d] = scores(*step[:3])

    def below_diagonal(u, carry):
        tiles = [n_sub * u + c for c in range(n_sub)]
        steps = [(h, cc, t, None) for t in tiles for h in range(2) for cc in range(n_chunks)]
        ahead = run(steps, [sc_ref[d] for d in range(depth)], head_steps(n_sub * (u + 1)))
        for d in range(depth):
            sc_ref[d] = ahead[d]
        return carry

    first = n_sub * i
    lax.fori_loop(0, i, below_diagonal, 0)

    steps = []
    for c in range(n_sub):
        for h in range(2):
            for cc in range(c * tk // cw, n_chunks):
                off = cc * cw - c * tk
                steps.append((h, cc, first + c, off if off < tk else None))
    assert [s[:2] for s in steps[:depth]] == [s[:2] for s in head_steps(first)]
    run(steps, [sc_ref[d] for d in range(depth)], [])

    acc0, acc1 = acc_ref[0], acc_ref[1]
    inv0 = 1.0 / acc0[HEAD_DIM:HEAD_DIM + 1]
    inv1 = 1.0 / acc1[0:1]
    out_t = jnp.concatenate([acc0[0:HEAD_DIM] * inv0, acc1[DEN_ROWS:] * inv1], axis=0)
    o_ref[0] = out_t.T.astype(BF16)


def _fox(qt, fqt, k, fk, vt, tq):
    B, n_tiles, tok_w, tk = qt.shape
    S = n_tiles * tk
    n_pairs = tok_w // LANES
    n_sub = tq // tk
    return pl.pallas_call(
        _fox_kernel,
        grid=(B, n_pairs, S // tq),
        in_specs=[pl.BlockSpec((1, n_sub, LANES, tk), lambda b, j, i: (b, i, j, 0)),
                  pl.BlockSpec((1, n_sub, LANES, tk), lambda b, j, i: (b, i, 0, 0)),
                  pl.BlockSpec((1, S, LANES), lambda b, j, i: (b, 0, j)),
                  pl.BlockSpec((1, S, LANES), lambda b, j, i: (b, 0, 0)),
                  pl.BlockSpec((1, n_tiles, LANES, tk), lambda b, j, i: (b, 0, j, 0))],
        out_specs=pl.BlockSpec((1, tq, LANES), lambda b, j, i: (b, i, j)),
        out_shape=jax.ShapeDtypeStruct((B, S, tok_w), BF16),
        scratch_shapes=[pltpu.VMEM((2, S, 2 * LANES), BF16),
                        pltpu.VMEM((2, n_tiles, HEAD_DIM + DEN_ROWS, tk), BF16),
                        pltpu.VMEM((2 * LANES, tq), BF16),
                        pltpu.VMEM((2, tk, 2 * MXU_N), F32),
                        pltpu.VMEM((2, 1, tq), F32),
                        pltpu.VMEM((2, HEAD_DIM + DEN_ROWS, tq), F32)],
        compiler_params=_params(3),
        name="fox",
    )(qt, fqt, k, fk, vt)


def _mix_b_kernel(x_ref, tok_ref, qm_ref, kb_ref, vb_ref, ones_ref, w_tok_ref, w_mem_ref, g_ref, b_ref, o_ref):
    m_len = kb_ref.shape[1] // MEM_HEADS
    ts = x_ref.shape[1]
    n_parts = MIX_PARTS if ts % (MIX_PARTS * SUBLANES) == 0 else 1
    pr = ts // n_parts
    for k in range(n_parts):
        rows = slice(k * pr, (k + 1) * pr)
        mem_out = _memory_attention(qm_ref[0, rows, :], kb_ref[0], vb_ref[0], ones_ref[...], m_len)
        mix = _dot(tok_ref[0, rows, :], w_tok_ref[...]) + _dot(mem_out.astype(BF16), w_mem_ref[...])
        o_ref[0, rows, :] = _layer_norm(DN_ALPHA * x_ref[0, rows, :] + mix, g_ref[...], b_ref[...])


def _mix_b(x, tok, qm, k_big, v_big, head_ones, w_tok, w_mem, g, b, ts):
    B, S, D = x.shape
    tok_w = tok.shape[2]
    mm = k_big.shape[1]
    return pl.pallas_call(
        _mix_b_kernel,
        grid=(B, S // ts),
        in_specs=[pl.BlockSpec((1, ts, D), lambda bi, si: (bi, si, 0)),
                  pl.BlockSpec((1, ts, tok_w), lambda bi, si: (bi, si, 0)),
                  pl.BlockSpec((1, ts, MEM_WIDTH), lambda bi, si: (bi, si, 0)),
                  pl.BlockSpec((1, mm, MEM_WIDTH), lambda bi, si: (bi, 0, 0)),
                  pl.BlockSpec((1, mm, MEM_WIDTH), lambda bi, si: (bi, 0, 0)),
                  _const_spec(head_ones.shape), _const_spec(w_tok.shape), _const_spec(w_mem.shape),
                  _const_spec((1, D)), _const_spec((1, D))],
        out_specs=pl.BlockSpec((1, ts, D), lambda bi, si: (bi, si, 0)),
        out_shape=jax.ShapeDtypeStruct((B, S, D), F32),
        compiler_params=_params(2),
        name="mix_b",
    )(x, tok, qm, k_big, v_big, head_ones, w_tok, w_mem, g, b)


def _prep_ffn(w_up, conv_w, conv_b, w_down, fc):
    f = w_up.shape[1] // 2
    fp = _round_up(f, fc)
    pad_cols = lambda a: jnp.pad(a, ((0, 0), (0, fp - f)))
    wu = pad_cols(w_up[:, :f]).astype(BF16)
    wg = pad_cols(w_up[:, f:]).astype(BF16)
    cp = pad_cols(jnp.concatenate([conv_w[:, :f], conv_w[:, f:], conv_b[None, :f], conv_b[None, f:]],
                                  axis=0))
    wd = jnp.pad(w_down, ((0, fp - f), (0, 0))).astype(BF16)
    return wu, wg, cp, wd


def _prep_mix_a(w_in, pool_w, pool_scale, w_out):
    d = w_in.shape[0]
    grp = pool_w.shape[1]
    tok_w = N_POOL * grp
    pad = POOL_PAD - grp
    w_tok = jnp.pad(w_in[:, :tok_w].reshape(d, N_POOL, grp), ((0, 0), (0, 0), (0, pad)))
    w_in_pad = jnp.concatenate([w_tok.reshape(d, N_POOL * POOL_PAD), w_in[:, tok_w:]], axis=1)
    pool_w_pad = jnp.pad(pool_w, ((0, 0), (0, pad), (0, pad)))
    pool_s_pad = jnp.pad(pool_scale.reshape(N_POOL, grp), ((0, 0), (0, pad)))
    o_tok = jnp.pad(w_out[:tok_w].reshape(N_POOL, grp, d), ((0, 0), (0, pad), (0, 0)))
    w_out_pad = jnp.concatenate([o_tok.reshape(N_POOL * POOL_PAD, d), w_out[tok_w:]], axis=0)
    return w_in_pad.astype(BF16), pool_w_pad.astype(BF16), pool_s_pad, w_out_pad.astype(BF16)


def _prep_qkv(w_q, kv_w, f_b, tok_w, ts):
    n_heads = tok_w // HEAD_DIM
    assert n_heads * DECAY_LANES <= LANES
    w_f = jnp.pad(kv_w[:, 2 * tok_w:], ((0, 0), (0, LANES - n_heads)))
    w_nat = jnp.concatenate([w_q[:, tok_w:], kv_w[:, :tok_w]], axis=1).astype(BF16)
    w_tr = jnp.concatenate([w_q[:, :tok_w] * (ATTN_SCALE * LOG2E), kv_w[:, tok_w:2 * tok_w], w_f],
                           axis=1).T.astype(BF16)
    fb_col = jnp.pad(f_b, (0, LANES - n_heads)).reshape(LANES, 1)
    tri = (jnp.arange(ts)[:, None] <= jnp.arange(ts)[None, :]).astype(BF16)
    src = jnp.arange(DECAY_PARTS * LANES)
    part, head = src // LANES, src % LANES
    dst = jnp.arange(LANES)
    valid = (head < n_heads)[:, None]
    pq = (valid & (dst[None, :] == (head * DECAY_LANES + part)[:, None])).astype(BF16)
    pk = -(valid & (dst[None, :] == (head * DECAY_LANES + DECAY_PARTS + part)[:, None])).astype(BF16)
    in_use = dst < n_heads * DECAY_LANES
    cq_col = (in_use & (dst % DECAY_LANES >= DECAY_PARTS)).astype(F32).reshape(LANES, 1)
    ck = (in_use & (dst % DECAY_LANES < DECAY_PARTS)).astype(F32).reshape(1, LANES)
    return w_nat, w_tr, fb_col, tri, pq.T, pk, cq_col, ck


ROW_TILE = 512
FFN_CHUNK = 256
FFN_ROW_TILE = 1024
MIX_ROW_TILE = 1024
MIX_PARTS = 2
FFN_SUBTILES = 1
ATTN_TILE = 4096


def kernel(x, mem, a_w_in, a_pool_w, a_pool_scale, a_w_out, b_w_q, b_w_out, kv_w, f_b, mem_w_kv,
           ln1_g, ln1_b, ln2_g, ln2_b, ffn_w_up, ffn_conv_w, ffn_conv_b, ffn_w_down):
    B, S, D = x.shape
    n_a = a_w_in.shape[0]
    n_b = b_w_q.shape[0]
    tok_w = D - MEM_WIDTH
    ts = min(ROW_TILE, S)
    tq = min(ATTN_TILE, S)
    assert S % ts == 0 and S % tq == 0 and tq % ts == 0 and tok_w % LANES == 0

    k_big, v_big = _memkv(mem, mem_w_kv.astype(BF16))
    m_len = mem.shape[1]
    head_ones = (jnp.arange(MEM_HEADS * m_len)[:, None] // m_len
                 == jnp.arange(MEM_WIDTH)[None, :] // HEAD_DIM).astype(BF16)
    row = lambda a, l: a[l].reshape(1, D)

    shared = None
    for l in range(n_a + n_b):
        if l < n_a:
            w_in_pad, pool_w_pad, pool_s_pad, w_out_pad = _prep_mix_a(
                a_w_in[l], a_pool_w[l], a_pool_scale[l], a_w_out[l])
            x = _mix_a(x, w_in_pad, pool_w_pad, pool_s_pad, k_big[l], v_big[l], head_ones, w_out_pad,
                       row(ln1_g, l), row(ln1_b, l), min(MIX_ROW_TILE, S))
        else:
            j = l - n_a
            qt, qm, k_new, vt_new, fqt_new, fk_new = _qkv(
                x, *_prep_qkv(b_w_q[j], kv_w, f_b, tok_w, ts), tok_w, ts)
            if j == 0:
                shared = (fqt_new, k_new, fk_new, vt_new)
            fqt, k_sh, fk_sh, vt_sh = shared
            tok = _fox(qt, fqt, k_sh, fk_sh, vt_sh, tq)
            w_out = b_w_out[j].astype(BF16)
            x = _mix_b(x, tok, qm, k_big[l], v_big[l], head_ones, w_out[:tok_w], w_out[tok_w:],
                       row(ln1_g, l), row(ln1_b, l), min(MIX_ROW_TILE, S))
        wu, wg, cp, wd = _prep_ffn(ffn_w_up[l], ffn_conv_w[l], ffn_conv_b[l], ffn_w_down[l], FFN_CHUNK)
        x = _ffn(x, wu, wg, cp, wd, row(ln2_g, l), row(ln2_b, l), min(FFN_ROW_TILE, S), FFN_CHUNK,
                 FFN_SUBTILES)
    return x
```
